```python
import jax, jax.numpy as jnp
from jax import lax
import numpy as np


D_MODEL = 1024
BATCH = 16
SEQ = 2048
DEPTH = 4

GRID_W = 64
CTX_LEN = 256
N_MIXERS = 4
EPS = 1e-6
ROPE_THETA = 10000.0
NEG_INF = -1e30
Q_BLOCK = 128
HEAD_DIM = 64
D_FF = -(-(8 * D_MODEL) // (3 * 256)) * 256
SWA_HEADS = D_MODEL // HEAD_DIM
SWA_KV_HEADS = 4
SWA_WINDOW = 128
NA_HEADS = D_MODEL // HEAD_DIM
NA_KH = 8
NA_KW = 16
NA_QCB = 16
NA_SLAB = 2 * NA_KW
POOL_WINDOWS = (2, 4, 8, 16)
POOL_GROUPS = len(POOL_WINDOWS)
POOL_DG = D_MODEL // POOL_GROUPS
MLA_HEADS = 16
MLA_NOPE = 64
MLA_ROPE = 32
MLA_V = 64
MLA_Q_RANK = 256
MLA_KV_RANK = 128

kernel_name = 'hybrid_interleaved_diffusion_trunk'


def n_layers_of(m):
    return len(range(m, DEPTH, N_MIXERS))


def rms_norm(x, g):
    xf = x.astype(jnp.float32)
    y = xf * lax.rsqrt(jnp.mean(xf * xf, axis=-1, keepdims=True) + EPS)
    return (y * g.astype(jnp.float32)).astype(x.dtype)


def modulate(x, g, shift, scale):
    return rms_norm(x, g) * (1 + scale) + shift


def rope_1d(x, pos):
    half = x.shape[-1] // 2
    freqs = ROPE_THETA ** (-jnp.arange(half, dtype=jnp.float32) / half)
    ang = pos.astype(jnp.float32)[:, None] * freqs[None, :]
    cos = jnp.cos(ang)[:, None, :]
    sin = jnp.sin(ang)[:, None, :]
    xf = x.astype(jnp.float32)
    x1, x2 = xf[..., :half], xf[..., half:]
    return jnp.concatenate([x1 * cos - x2 * sin, x1 * sin + x2 * cos], -1).astype(x.dtype)


def rope_2d_axial(x, length):
    t = jnp.arange(length)
    d = x.shape[-1] // 2
    return jnp.concatenate([rope_1d(x[..., :d], t // GRID_W), rope_1d(x[..., d:], t % GRID_W)], -1)


def softmax_f32(s, sink=None):
    if sink is None:
        return jax.nn.softmax(s, axis=-1)
    sink = jnp.broadcast_to(sink, s.shape[:-1] + (1,))
    return jax.nn.softmax(jnp.concatenate([s, sink], -1), axis=-1)[..., :-1]


def swiglu(h, w_gu, w_down):
    gu = h @ w_gu
    return (jax.nn.silu(gu[..., :D_FF]) * gu[..., D_FF:]) @ w_down


def swa_mixer(a_lat, a_ctx, w_qkv, g_q, g_k, sink, w_o, need_ctx):
    B, L, _ = a_lat.shape
    Lc = a_ctx.shape[1]
    Hq, Hk, dh = SWA_HEADS, SWA_KV_HEADS, HEAD_DIM
    G = Hq // Hk
    nq, nk = Hq * dh, Hk * dh
    scale = dh ** -0.5
    qkv = a_lat @ w_qkv
    q = rope_2d_axial(rms_norm(qkv[..., :nq].reshape(B, L, Hq, dh), g_q), L)
    k = rope_2d_axial(rms_norm(qkv[..., nq:nq + nk].reshape(B, L, Hk, dh), g_k), L)
    v = qkv[..., nq + nk:].reshape(B, L, Hk, dh)
    kv_c = a_ctx @ w_qkv[:, nq:]
    k_c = rms_norm(kv_c[..., :nk].reshape(B, Lc, Hk, dh), g_k)
    v_c = kv_c[..., nk:].reshape(B, Lc, Hk, dh)
    sink_g = sink.astype(jnp.float32).reshape(Hk, G)[None, :, :, None, None]
    nb = L // Q_BLOCK
    span = Q_BLOCK + 2 * SWA_WINDOW
    pad = ((0, 0), (SWA_WINDOW, SWA_WINDOW), (0, 0), (0, 0))
    k_p = jnp.pad(k, pad)
    v_p = jnp.pad(v, pad)
    q_blocks = jnp.moveaxis(q.reshape(B, nb, Q_BLOCK, Hk, G, dh), 1, 0)
    ctx_valid = jnp.ones((Q_BLOCK, Lc), bool)

    def block(args):
        b, q_b = args
        start = b * Q_BLOCK
        k_b = jnp.concatenate([lax.dynamic_slice_in_dim(k_p, start, span, 1), k_c], 1)
        v_b = jnp.concatenate([lax.dynamic_slice_in_dim(v_p, start, span, 1), v_c], 1)
        q_pos = start + jnp.arange(Q_BLOCK)
        k_pos = start - SWA_WINDOW + jnp.arange(span)
        valid = (jnp.abs(q_pos[:, None] - k_pos[None, :]) <= SWA_WINDOW) & (k_pos >= 0) & (k_pos < L)
        valid = jnp.concatenate([valid, ctx_valid], 1)
        s = jnp.einsum('bqhgd,bkhd->bhgqk', q_b, k_b, preferred_element_type=jnp.float32) * scale
        p = softmax_f32(jnp.where(valid, s, NEG_INF), sink_g).astype(v_b.dtype)
        return jnp.einsum('bhgqk,bkhd->bqhgd', p, v_b)

    o = lax.map(block, (jnp.arange(nb), q_blocks))
    y_lat = jnp.moveaxis(o, 0, 1).reshape(B, L, nq) @ w_o
    y_ctx = None
    if need_ctx:
        q_c = rms_norm((a_ctx @ w_qkv[:, :nq]).reshape(B, Lc, Hk, G, dh), g_q)
        s = jnp.einsum('bqhgd,bkhd->bhgqk', q_c, k_c, preferred_element_type=jnp.float32) * scale
        p = softmax_f32(s, sink_g).astype(v_c.dtype)
        y_ctx = jnp.einsum('bhgqk,bkhd->bqhgd', p, v_c).reshape(B, Lc, nq) @ w_o
    return y_lat, y_ctx


def na_mixer(a_lat, a_ctx, w_qkv, g_q, g_k, rpb, w_o, need_ctx):
    B, L, _ = a_lat.shape
    Lc = a_ctx.shape[1]
    H, dh = NA_HEADS, HEAD_DIM
    n = H * dh
    scale = dh ** -0.5
    rows = L // GRID_W
    kh = min(NA_KH, rows)
    qkv = a_lat @ w_qkv
    q = rms_norm(qkv[..., :n].reshape(B, rows, GRID_W, H, dh), g_q)
    k = rms_norm(qkv[..., n:2 * n].reshape(B, rows, GRID_W, H, dh), g_k)
    v = qkv[..., 2 * n:].reshape(B, rows, GRID_W, H, dh)
    kv_c = a_ctx @ w_qkv[:, n:]
    k_c = rms_norm(kv_c[..., :n].reshape(B, Lc, H, dh), g_k)
    v_c = kv_c[..., n:].reshape(B, Lc, H, dh)
    ncb = GRID_W // NA_QCB
    q_col = np.arange(GRID_W).reshape(ncb, NA_QCB)
    slab0 = [int(s) for s in np.clip(np.arange(ncb) * NA_QCB - NA_KW // 2, 0, GRID_W - NA_SLAB)]
    col_start = np.clip(q_col - NA_KW // 2, 0, GRID_W - NA_KW)
    key_col = np.asarray(slab0)[:, None] + np.arange(NA_SLAB)
    kc3 = key_col[:, None, :]
    col_valid = (kc3 >= col_start[..., None]) & (kc3 < col_start[..., None] + NA_KW)
    dc_idx = (np.clip(kc3 - q_col[..., None], -(NA_KW - 1), NA_KW - 1) + NA_KW - 1)[:, :, None, :]
    nkey = kh * NA_SLAB
    valid_lat = np.broadcast_to(col_valid[:, :, None, :], (ncb, NA_QCB, kh, NA_SLAB)).reshape(ncb, NA_QCB, nkey)
    valid = jnp.asarray(np.concatenate([valid_lat, np.ones((ncb, NA_QCB, Lc), bool)], -1))
    rpb_f = rpb.astype(jnp.float32)

    def slabs(t):
        return jnp.stack([t[:, :, s:s + NA_SLAB] for s in slab0], 1).reshape(B, ncb, nkey, H, dh)

    def row_step(args):
        r, q_r = args
        r0 = jnp.clip(r - kh // 2, 0, rows - kh)
        k_s = slabs(lax.dynamic_slice_in_dim(k, r0, kh, 1))
        v_s = slabs(lax.dynamic_slice_in_dim(v, r0, kh, 1))
        dr_idx = (r0 + jnp.arange(kh) - r + NA_KH - 1)[None, None, :, None]
        bias = rpb_f[:, dr_idx, dc_idx].reshape(H, ncb, NA_QCB, nkey)
        q_b = q_r.reshape(B, ncb, NA_QCB, H, dh)
        s_lat = jnp.einsum('bjqhd,bjkhd->bhjqk', q_b, k_s, preferred_element_type=jnp.float32) * scale + bias
        s_ctx = jnp.einsum('bjqhd,bkhd->bhjqk', q_b, k_c, preferred_element_type=jnp.float32) * scale
        p = softmax_f32(jnp.where(valid, jnp.concatenate([s_lat, s_ctx], -1), NEG_INF)).astype(v.dtype)
        o = (jnp.einsum('bhjqk,bjkhd->bjqhd', p[..., :nkey], v_s)
             + jnp.einsum('bhjqk,bkhd->bjqhd', p[..., nkey:], v_c))
        return o.reshape(B, GRID_W, H, dh)

    o = lax.map(row_step, (jnp.arange(rows), jnp.moveaxis(q, 1, 0)))
    y_lat = jnp.moveaxis(o, 0, 1).reshape(B, L, n) @ w_o
    y_ctx = None
    if need_ctx:
        q_c = rms_norm((a_ctx @ w_qkv[:, :n]).reshape(B, Lc, H, dh), g_q)
        s = jnp.einsum('bqhd,bkhd->bhqk', q_c, k_c, preferred_element_type=jnp.float32) * scale
        p = softmax_f32(s).astype(v_c.dtype)
        y_ctx = jnp.einsum('bhqk,bkhd->bqhd', p, v_c).reshape(B, Lc, n) @ w_o
    return y_lat, y_ctx


def multi_pool(h):
    B, L, _ = h.shape
    hf = h.astype(jnp.float32)
    cs = jnp.pad(lax.cumsum(hf, axis=1), ((0, 0), (1, 0), (0, 0)))
    t = jnp.arange(L)
    outs = []
    for g, w in enumerate(POOL_WINDOWS):
        lo = jnp.clip(t - w // 2, 0, L)
        hi = jnp.clip(t - w // 2 + w, 0, L)
        cg = cs[..., g * POOL_DG:(g + 1) * POOL_DG]
        outs.append((cg[:, hi] - cg[:, lo]) / (hi - lo).astype(jnp.float32)[None, :, None])
    return (jnp.concatenate(outs, -1) - hf).astype(h.dtype)


def pool_mixer(a_lat, a_ctx, w, b, scale, need_ctx):
    def mix(h):
        B, L, _ = h.shape
        p = multi_pool(h).reshape(B, L, POOL_GROUPS, POOL_DG)
        y = jnp.einsum('blgc,gcd->blgd', p, w) + b
        return y.reshape(B, L, D_MODEL) * scale
    return mix(a_lat), (mix(a_ctx) if need_ctx else None)


def mla_mixer(a_lat, a_ctx, w_a, g_cq, g_ckv, w_uq, w_ukv, g_q, g_k, w_o, need_ctx):
    H, dn, dr, dv = MLA_HEADS, MLA_NOPE, MLA_ROPE, MLA_V
    dqk = dn + dr
    qr = MLA_Q_RANK
    scale = dqk ** -0.5

    def queries(cq_raw):
        B_, L_, _ = cq_raw.shape
        c_q = rms_norm(cq_raw, g_cq)
        return rms_norm((c_q @ w_uq).reshape(B_, L_, H, dqk), g_q)

    def keys_values(kv_raw):
        B_, L_, _ = kv_raw.shape
        c_kv = rms_norm(kv_raw[..., :MLA_KV_RANK], g_ckv)
        k_r = jnp.broadcast_to(kv_raw[..., MLA_KV_RANK:][:, :, None, :], (B_, L_, H, dr))
        kv = (c_kv @ w_ukv).reshape(B_, L_, H, dn + dv)
        k = rms_norm(jnp.concatenate([kv[..., :dn], k_r], -1), g_k)
        return k, kv[..., dn:]

    def rope_part(t):
        return jnp.concatenate([t[..., :dn], rope_2d_axial(t[..., dn:], t.shape[1])], -1)

    B, L, _ = a_lat.shape
    Lc = a_ctx.shape[1]
    proj = a_lat @ w_a
    q = rope_part(queries(proj[..., :qr]))
    k, v = keys_values(proj[..., qr:])
    k = rope_part(k)
    if need_ctx:
        proj_c = a_ctx @ w_a
        q_c = queries(proj_c[..., :qr])
        kv_raw_c = proj_c[..., qr:]
    else:
        kv_raw_c = a_ctx @ w_a[:, qr:]
    k_c, v_c = keys_values(kv_raw_c)
    k_all = jnp.concatenate([k, k_c], 1)
    v_all = jnp.concatenate([v, v_c], 1)
    nb = L // Q_BLOCK
    q_blocks = jnp.moveaxis(q.reshape(B, nb, Q_BLOCK, H, dqk), 1, 0)

    def block(q_b):
        s = jnp.einsum('bqhd,bkhd->bhqk', q_b, k_all, preferred_element_type=jnp.float32) * scale
        p = softmax_f32(s).astype(v_all.dtype)
        return jnp.einsum('bhqk,bkhd->bqhd', p, v_all)

    o = lax.map(block, q_blocks)
    y_lat = jnp.moveaxis(o, 0, 1).reshape(B, L, H * dv) @ w_o
    y_ctx = None
    if need_ctx:
        s = jnp.einsum('bqhd,bkhd->bhqk', q_c, k_c, preferred_element_type=jnp.float32) * scale
        p = softmax_f32(s).astype(v_c.dtype)
        y_ctx = jnp.einsum('bhqk,bkhd->bqhd', p, v_c).reshape(B, Lc, H * dv) @ w_o
    return y_lat, y_ctx


def setup_inputs(seed: int = 0) -> dict:
    key = jax.random.key(seed)
    ks = iter(jax.random.split(key, 48))
    f32 = jnp.float32

    def nrm(shape, fan_in, gain=1.0):
        return jax.random.normal(next(ks), shape, f32) * (gain * fan_in ** -0.5)

    def gains(shape):
        return 1.0 + 0.05 * jax.random.normal(next(ks), shape, f32)

    def small(shape, s=0.01):
        return s * jax.random.normal(next(ks), shape, f32)

    D = D_MODEL
    nA, nB, nC, nD = (n_layers_of(m) for m in range(N_MIXERS))
    dqk = MLA_NOPE + MLA_ROPE
    return {
        'x': jax.random.normal(next(ks), (BATCH, SEQ, D), f32),
        'c': jax.random.normal(next(ks), (BATCH, D), f32),
        'ctx': jax.random.normal(next(ks), (BATCH, CTX_LEN, D), f32),
        'c_ctx': jax.random.normal(next(ks), (D,), f32),
        'w_ada': nrm((DEPTH, D, 6 * D), D, 0.5),
        'b_ada': small((DEPTH, 6 * D)),
        'g_mix': gains((DEPTH, D)),
        'g_ffn': gains((DEPTH, D)),
        'w_gate_up': nrm((DEPTH, D, 2 * D_FF), D),
        'w_down': nrm((DEPTH, D_FF, D), D_FF),
        'swa_w_qkv': nrm((nA, D, (SWA_HEADS + 2 * SWA_KV_HEADS) * HEAD_DIM), D),
        'swa_g_q': gains((nA, HEAD_DIM)),
        'swa_g_k': gains((nA, HEAD_DIM)),
        'swa_sink': small((nA, SWA_HEADS), 0.5),
        'swa_w_o': nrm((nA, SWA_HEADS * HEAD_DIM, D), SWA_HEADS * HEAD_DIM),
        'na_w_qkv': nrm((nB, D, 3 * NA_HEADS * HEAD_DIM), D),
        'na_g_q': gains((nB, HEAD_DIM)),
        'na_g_k': gains((nB, HEAD_DIM)),
        'na_rpb': small((nB, NA_HEADS, 2 * NA_KH - 1, 2 * NA_KW - 1), 0.1),
        'na_w_o': nrm((nB, NA_HEADS * HEAD_DIM, D), NA_HEADS * HEAD_DIM),
        'pool_w': nrm((nC, POOL_GROUPS, POOL_DG, POOL_DG), POOL_DG),
        'pool_b': small((nC, POOL_GROUPS, POOL_DG)),
        'pool_scale': gains((nC, D)),
        'mla_w_a': nrm((nD, D, MLA_Q_RANK + MLA_KV_RANK + MLA_ROPE), D),
        'mla_g_cq': gains((nD, MLA_Q_RANK)),
        'mla_g_ckv': gains((nD, MLA_KV_RANK)),
        'mla_w_uq': nrm((nD, MLA_Q_RANK, MLA_HEADS * dqk), MLA_Q_RANK),
        'mla_w_ukv': nrm((nD, MLA_KV_RANK, MLA_HEADS * (MLA_NOPE + MLA_V)), MLA_KV_RANK),
        'mla_g_q': gains((nD, dqk)),
        'mla_g_k': gains((nD, dqk)),
        'mla_w_o': nrm((nD, MLA_HEADS * MLA_V, D), MLA_HEADS * MLA_V),
    }


def reference(x, c, ctx, c_ctx, w_ada, b_ada, g_mix, g_ffn, w_gate_up, w_down,
              swa_w_qkv, swa_g_q, swa_g_k, swa_sink, swa_w_o,
              na_w_qkv, na_g_q, na_g_k, na_rpb, na_w_o,
              pool_w, pool_b, pool_scale,
              mla_w_a, mla_g_cq, mla_g_ckv, mla_w_uq, mla_w_ukv, mla_g_q, mla_g_k, mla_w_o):
    cond = jax.nn.silu(c)
    cond_c = jax.nn.silu(c_ctx)
    h, hc = x, ctx
    for i in range(DEPTH):
        m, j = i % N_MIXERS, i // N_MIXERS
        need_ctx = i < DEPTH - 1
        mod = (cond @ w_ada[i] + b_ada[i])[:, None, :]
        mod_c = cond_c @ w_ada[i] + b_ada[i]
        sh1, sc1, gt1, sh2, sc2, gt2 = jnp.split(mod, 6, axis=-1)
        sh1c, sc1c, gt1c, sh2c, sc2c, gt2c = jnp.split(mod_c, 6, axis=-1)
        a = modulate(h, g_mix[i], sh1, sc1)
        ac = modulate(hc, g_mix[i], sh1c, sc1c)
        if m == 0:
            y, yc = swa_mixer(a, ac, swa_w_qkv[j], swa_g_q[j], swa_g_k[j], swa_sink[j], swa_w_o[j], need_ctx)
        elif m == 1:
            y, yc = na_mixer(a, ac, na_w_qkv[j], na_g_q[j], na_g_k[j], na_rpb[j], na_w_o[j], need_ctx)
        elif m == 2:
            y, yc = pool_mixer(a, ac, pool_w[j], pool_b[j], pool_scale[j], need_ctx)
        else:
            y, yc = mla_mixer(a, ac, mla_w_a[j], mla_g_cq[j], mla_g_ckv[j], mla_w_uq[j], mla_w_ukv[j],
                              mla_g_q[j], mla_g_k[j], mla_w_o[j], need_ctx)
        h = h + gt1 * y
        h = h + gt2 * swiglu(modulate(h, g_ffn[i], sh2, sc2), w_gate_up[i], w_down[i])
        if need_ctx:
            hc = hc + gt1c * yc
            hc = hc + gt2c * swiglu(modulate(hc, g_ffn[i], sh2c, sc2c), w_gate_up[i], w_down[i])
    return h
```

```python
import functools

import numpy as np
import jax
import jax.numpy as jnp
from jax import lax
from jax.experimental import pallas as pl
from jax.experimental.pallas import tpu as pltpu

F32 = jnp.float32
BF16 = jnp.bfloat16

D = 1024
DEPTH = 4
GRID_W = 64
EPS = 1e-6
ROPE_THETA = 10000.0
NEG_INF = -1e30
HEAD_DIM = 64
D_FF = 2816
SWA_HEADS = 16
SWA_KV_HEADS = 4
SWA_GROUPS = SWA_HEADS // SWA_KV_HEADS
SWA_WINDOW = 128
NA_HEADS = 16
NA_KH = 8
NA_KW = 16
POOL_WINDOWS = (2, 4, 8, 16)
POOL_DG = D // len(POOL_WINDOWS)
MLA_HEADS = 16
MLA_NOPE = 64
MLA_ROPE = 32
MLA_V = 64
MLA_QK = MLA_NOPE + MLA_ROPE
MLA_Q_RANK = 256
MLA_KV_RANK = 128

LANES = 128
SUBLANES = 8
VMEM_LIMIT = 56 * 1024 * 1024

NA_QROWS = 4
NA_WROWS = 12
NA_QB = NA_QROWS * GRID_W
NA_WB = NA_WROWS * GRID_W


def _cparams(n_axes):
    return pltpu.CompilerParams(dimension_semantics=("parallel",) * n_axes,
                                vmem_limit_bytes=VMEM_LIMIT)


def _const_spec(shape):
    nd = len(shape)
    return pl.BlockSpec(shape, lambda *_: (0,) * nd, pipeline_mode=pl.Buffered(1))


def _dot(a, b):
    return jnp.dot(a, b, preferred_element_type=F32)


def _dot_nt(a, b):
    return lax.dot_general(a, b, (((1,), (1,)), ((), ())), preferred_element_type=F32)


def _silu(x):
    return x / (1.0 + jnp.exp(-x))


def _lane_iota():
    return lax.broadcasted_iota(jnp.int32, (1, LANES), 1)


def _modnorm(h, g, shift, scale):
    ms = jnp.mean(h * h, axis=-1, keepdims=True)
    y = h * lax.rsqrt(ms + EPS) * g
    return y * (1.0 + scale) + shift


def _rms_full(x, g, n):
    ms = jnp.sum(x * x, axis=-1, keepdims=True) * (1.0 / n)
    return x * lax.rsqrt(ms + EPS) * g


def _rms_head64(y, g):
    lo = _lane_iota() < HEAD_DIM
    x2 = y * y
    sa = jnp.sum(jnp.where(lo, x2, 0.0), axis=-1, keepdims=True)
    sb = jnp.sum(jnp.where(lo, 0.0, x2), axis=-1, keepdims=True)
    ms = jnp.where(lo, sa, sb) * (1.0 / HEAD_DIM)
    return y * lax.rsqrt(ms + EPS) * g


def _rope(y, cos, sin, half):
    first = (_lane_iota() % (2 * half)) < half
    sw = jnp.where(first, pltpu.roll(y, LANES - half, 1), pltpu.roll(y, half, 1))
    return y * cos + sw * sin


def _dup_halves(y):
    lo = _lane_iota() < HEAD_DIM
    yr = pltpu.roll(y, HEAD_DIM, 1)
    return jnp.where(lo, y, yr), jnp.where(lo, yr, y)


def _softmax_pv(parts, vals, sink=None):
    m = None
    for s in parts:
        mi = jnp.max(s, axis=-1, keepdims=True)
        m = mi if m is None else jnp.maximum(m, mi)
    if sink is not None:
        m = jnp.maximum(m, sink)
    den = None
    o = None
    for s, v in zip(parts, vals):
        e = jnp.exp(s - m)
        di = jnp.sum(e, axis=-1, keepdims=True)
        oi = _dot(e.astype(BF16), v)
        den = di if den is None else den + di
        o = oi if o is None else o + oi
    if sink is not None:
        den = den + jnp.exp(sink - m)
    return o / den


def _ada_kernel(c_ref, w_ref, b_ref, o_ref):
    cond = _silu(c_ref[...]).astype(BF16)
    o_ref[0] = _dot(cond, w_ref[0].astype(BF16)) + b_ref[0]


def _ada(cvec, w_ada, b_ada):
    rows = cvec.shape[0]
    nblk = 6
    return pl.pallas_call(
        _ada_kernel,
        out_shape=jax.ShapeDtypeStruct((DEPTH, rows, 6 * D), F32),
        grid=(DEPTH, nblk),
        in_specs=[pl.BlockSpec((rows, D), lambda i, j: (0, 0)),
                  pl.BlockSpec((1, D, D), lambda i, j: (i, 0, j)),
                  pl.BlockSpec((1, 1, D), lambda i, j: (i, 0, j))],
        out_specs=pl.BlockSpec((1, rows, D), lambda i, j: (i, 0, j)),
        compiler_params=_cparams(2),
        name="ada",
    )(cvec, w_ada, b_ada.reshape(DEPTH, 1, 6 * D))


def _mod_spec(mod, batch_axis):
    if mod.shape[0] == 1:
        return pl.BlockSpec((1, 6, D), lambda *ids: (0, 0, 0))
    return pl.BlockSpec((1, 6, D), lambda *ids: (ids[batch_axis], 0, 0))


def _row_tile(L):
    return min(L, 512)


FFN_CHUNK = 256


def _ffn_kernel(h_ref, mod_ref, g_ref, wgu_ref, wd_ref, o_ref, act_ref):
    h = h_ref[0]
    a = _modnorm(h, g_ref[...], mod_ref[0, 3:4, :], mod_ref[0, 4:5, :]).astype(BF16)
    for c in range(D_FF // FFN_CHUNK):
        lo = c * FFN_CHUNK
        gate = _dot(a, wgu_ref[:, lo:lo + FFN_CHUNK])
        up = _dot(a, wgu_ref[:, D_FF + lo:D_FF + lo + FFN_CHUNK])
        act_ref[:, lo:lo + FFN_CHUNK] = (_silu(gate) * up).astype(BF16)
    y = _dot(act_ref[...], wd_ref[...])
    o_ref[0] = h + mod_ref[0, 5:6, :] * y


def _ffn(h, mod, g, wgu, wd):
    B, L, _ = h.shape
    tm = _row_tile(L)
    return pl.pallas_call(
        _ffn_kernel,
        out_shape=jax.ShapeDtypeStruct(h.shape, F32),
        grid=(B, L // tm),
        in_specs=[pl.BlockSpec((1, tm, D), lambda b, i: (b, i, 0)),
                  _mod_spec(mod, 0),
                  _const_spec((1, D)),
                  _const_spec((D, 2 * D_FF)),
                  _const_spec((D_FF, D))],
        out_specs=pl.BlockSpec((1, tm, D), lambda b, i: (b, i, 0)),
        scratch_shapes=[pltpu.VMEM((tm, D_FF), BF16)],
        compiler_params=_cparams(2),
        name="ffn",
    )(h, mod, g.reshape(1, D), wgu, wd)


def _oproj_kernel(o_ref, w_ref, h_ref, mod_ref, out_ref):
    out_ref[0] = h_ref[0] + mod_ref[0, 2:3, :] * _dot(o_ref[0], w_ref[...])


def _oproj(o, w, h, mod):
    B, L, _ = h.shape
    tm = _row_tile(L)
    return pl.pallas_call(
        _oproj_kernel,
        out_shape=jax.ShapeDtypeStruct(h.shape, F32),
        grid=(B, L // tm),
        in_specs=[pl.BlockSpec((1, tm, D), lambda b, i: (b, i, 0)),
                  _const_spec((D, D)),
                  pl.BlockSpec((1, tm, D), lambda b, i: (b, i, 0)),
                  _mod_spec(mod, 0)],
        out_specs=pl.BlockSpec((1, tm, D), lambda b, i: (b, i, 0)),
        compiler_params=_cparams(2),
        name="oproj",
    )(o, w, h, mod)


def _axial_tables(L, half, lead, width):
    t = np.arange(L)
    freqs = ROPE_THETA ** (-np.arange(half, dtype=np.float64) / half)
    cos = np.ones((L, LANES), np.float64)
    sin = np.zeros((L, LANES), np.float64)
    for base in range(0, LANES, width):
        for axis, pos in enumerate((t // GRID_W, t % GRID_W)):
            ang = pos[:, None].astype(np.float64) * freqs[None, :]
            o = base + lead + axis * 2 * half
            cos[:, o:o + half] = np.cos(ang)
            cos[:, o + half:o + 2 * half] = np.cos(ang)
            sin[:, o:o + half] = -np.sin(ang)
            sin[:, o + half:o + 2 * half] = np.sin(ang)
    return jnp.asarray(cos, F32), jnp.asarray(sin, F32)


SWA_NQ = SWA_HEADS * HEAD_DIM
SWA_NK = SWA_KV_HEADS * HEAD_DIM
SWA_SCALE = HEAD_DIM ** -0.5


def _swa_proj_kernel(h_ref, mod_ref, g_ref, w_ref, gq_ref, gk_ref, cos_ref, sin_ref,
                     q_ref, k_ref, v_ref, *, rope):
    a = _modnorm(h_ref[0], g_ref[...], mod_ref[0, 0:1, :], mod_ref[0, 1:2, :]).astype(BF16)
    y = _dot(a, w_ref[...])
    if rope:
        cos, sin = cos_ref[...], sin_ref[...]
    for cb in range(SWA_NQ // LANES):
        blk = _rms_head64(y[:, cb * LANES:(cb + 1) * LANES], gq_ref[...])
        if rope:
            blk = _rope(blk, cos, sin, HEAD_DIM // 4)
        q_ref[0, :, cb * LANES:(cb + 1) * LANES] = (blk * SWA_SCALE).astype(BF16)
    for p in range(SWA_NK // LANES):
        blk = _rms_head64(y[:, SWA_NQ + p * LANES:SWA_NQ + (p + 1) * LANES], gk_ref[...])
        if rope:
            blk = _rope(blk, cos, sin, HEAD_DIM // 4)
        ka, kb = _dup_halves(blk)
        k_ref[0, :, (2 * p) * LANES:(2 * p + 1) * LANES] = ka.astype(BF16)
        k_ref[0, :, (2 * p + 1) * LANES:(2 * p + 2) * LANES] = kb.astype(BF16)
        va, vb = _dup_halves(y[:, SWA_NQ + SWA_NK + p * LANES:SWA_NQ + SWA_NK + (p + 1) * LANES])
        v_ref[0, :, (2 * p) * LANES:(2 * p + 1) * LANES] = va.astype(BF16)
        v_ref[0, :, (2 * p + 1) * LANES:(2 * p + 2) * LANES] = vb.astype(BF16)


def _swa_proj(h, mod, g, w, gq, gk, cos, sin, rope):
    B, L, _ = h.shape
    tm = _row_tile(L)
    nkv = 2 * SWA_NK
    row = lambda b, i: (b, i, 0)
    return pl.pallas_call(
        functools.partial(_swa_proj_kernel, rope=rope),
        out_shape=(jax.ShapeDtypeStruct((B, L, SWA_NQ), BF16),
                   jax.ShapeDtypeStruct((B, L, nkv), BF16),
                   jax.ShapeDtypeStruct((B, L, nkv), BF16)),
        grid=(B, L // tm),
        in_specs=[pl.BlockSpec((1, tm, D), row),
                  _mod_spec(mod, 0),
                  _const_spec((1, D)),
                  _const_spec((D, SWA_NQ + 2 * SWA_NK)),
                  _const_spec((1, LANES)),
                  _const_spec((1, LANES)),
                  pl.BlockSpec((tm, LANES), lambda b, i: (i, 0)),
                  pl.BlockSpec((tm, LANES), lambda b, i: (i, 0))],
        out_specs=(pl.BlockSpec((1, tm, SWA_NQ), row),
                   pl.BlockSpec((1, tm, nkv), row),
                   pl.BlockSpec((1, tm, nkv), row)),
        compiler_params=_cparams(2),
        name="swa_proj",
    )(h, mod, g.reshape(1, D), w, jnp.tile(gq, 2).reshape(1, LANES), jnp.tile(gk, 2).reshape(1, LANES),
      cos[:L], sin[:L])


SWA_QB = 128
SWA_SPAN = SWA_QB + 2 * SWA_WINDOW


def _swa_attn_kernel(*refs, with_window, L):
    if with_window:
        q_ref, kd_ref, vd_ref, kc_ref, vc_ref, sink_ref, o_ref = refs
    else:
        q_ref, kc_ref, vc_ref, sink_ref, o_ref = refs
    lo = _lane_iota() < HEAD_DIM
    rows = SWA_GROUPS * SWA_QB
    if with_window:
        start = pl.program_id(1) * SWA_QB
        w0 = pl.multiple_of(jnp.clip(start - SWA_WINDOW, 0, L - SWA_SPAN), SWA_QB)
        qpos = lax.broadcasted_iota(jnp.int32, (rows, SWA_SPAN), 0) % SWA_QB
        kpos = lax.broadcasted_iota(jnp.int32, (rows, SWA_SPAN), 1)
        valid = jnp.abs(qpos - kpos + (start - w0)) <= SWA_WINDOW
    for hk in range(SWA_KV_HEADS):
        qs = []
        for g in range(SWA_GROUPS):
            cb = hk * 2 + g // 2
            qc = q_ref[0, :, cb * LANES:(cb + 1) * LANES]
            zero = jnp.zeros_like(qc)
            qs.append(jnp.where(lo, qc, zero) if g % 2 == 0 else jnp.where(lo, zero, qc))
        qstack = jnp.concatenate(qs, axis=0)
        sink = jnp.concatenate(
            [jnp.full((SWA_QB, 1), sink_ref[hk * SWA_GROUPS + g], F32) for g in range(SWA_GROUPS)], axis=0)
        cols = slice(hk * LANES, (hk + 1) * LANES)
        parts = [_dot_nt(qstack, kc_ref[0, :, cols])]
        vals = [vc_ref[0, :, cols]]
        if with_window:
            s_w = _dot_nt(qstack, kd_ref[0, pl.ds(w0, SWA_SPAN), cols])
            parts.insert(0, jnp.where(valid, s_w, NEG_INF))
            vals.insert(0, vd_ref[0, pl.ds(w0, SWA_SPAN), cols])
        o = _softmax_pv(parts, vals, sink)
        for j in range(2):
            oa = o[(2 * j) * SWA_QB:(2 * j + 1) * SWA_QB]
            ob = o[(2 * j + 1) * SWA_QB:(2 * j + 2) * SWA_QB]
            o_ref[0, :, (hk * 2 + j) * LANES:(hk * 2 + j + 1) * LANES] = jnp.where(lo, oa, ob).astype(BF16)


def _swa_attn(q, kd, vd, kcd, vcd, sink):
    B, L, _ = q.shape
    Lc = kcd.shape[1]
    nkv = 2 * SWA_NK
    with_window = kd is not None
    blk = lambda b, i: (b, i, 0)
    whole = lambda b, i: (b, 0, 0)
    specs = [pl.BlockSpec((1, SWA_QB, SWA_NQ), blk)]
    args = [q]
    if with_window:
        specs += [pl.BlockSpec((1, L, nkv), whole), pl.BlockSpec((1, L, nkv), whole)]
        args += [kd, vd]
    specs += [pl.BlockSpec((1, Lc, nkv), whole), pl.BlockSpec((1, Lc, nkv), whole),
              pl.BlockSpec(memory_space=pltpu.SMEM)]
    args += [kcd, vcd, sink]
    return pl.pallas_call(
        functools.partial(_swa_attn_kernel, with_window=with_window, L=L),
        out_shape=jax.ShapeDtypeStruct((B, L, SWA_NQ), BF16),
        grid=(B, L // SWA_QB),
        in_specs=specs,
        out_specs=pl.BlockSpec((1, SWA_QB, SWA_NQ), blk),
        compiler_params=_cparams(2),
        name="swa_attn" if with_window else "swa_attn_ctx",
    )(*args)


NA_N = NA_HEADS * HEAD_DIM
NA_SCALE = HEAD_DIM ** -0.5


def _na_proj_kernel(h_ref, mod_ref, g_ref, w_ref, gq_ref, gk_ref, q_ref, k_ref, v_ref):
    a = _modnorm(h_ref[0], g_ref[...], mod_ref[0, 0:1, :], mod_ref[0, 1:2, :]).astype(BF16)
    y = _dot(a, w_ref[...])
    for cb in range(NA_N // LANES):
        cols = slice(cb * LANES, (cb + 1) * LANES)
        q_ref[0, :, cols] = (_rms_head64(y[:, cols], gq_ref[...]) * NA_SCALE).astype(BF16)
        k_ref[0, :, cols] = _rms_head64(y[:, NA_N + cb * LANES:NA_N + (cb + 1) * LANES], gk_ref[...]).astype(BF16)
        v_ref[0, :, cols] = y[:, 2 * NA_N + cb * LANES:2 * NA_N + (cb + 1) * LANES].astype(BF16)


def _na_proj(h, mod, g, w, gq, gk):
    B, L, _ = h.shape
    tm = _row_tile(L)
    row = lambda b, i: (b, i, 0)
    out = jax.ShapeDtypeStruct((B, L, NA_N), BF16)
    return pl.pallas_call(
        _na_proj_kernel,
        out_shape=(out, out, out),
        grid=(B, L // tm),
        in_specs=[pl.BlockSpec((1, tm, D), row),
                  _mod_spec(mod, 0),
                  _const_spec((1, D)),
                  _const_spec((D, 3 * NA_N)),
                  _const_spec((1, LANES)),
                  _const_spec((1, LANES))],
        out_specs=(pl.BlockSpec((1, tm, NA_N), row),) * 3,
        compiler_params=_cparams(2),
        name="na_proj",
    )(h, mod, g.reshape(1, D), w, jnp.tile(gq, 2).reshape(1, LANES), jnp.tile(gk, 2).reshape(1, LANES))


def _na_bias_tables(rpb, rows):
    kh = min(NA_KH, rows)
    nblk = rows // NA_QROWS
    tables = []
    c = np.arange(GRID_W)
    col_start = np.clip(c - NA_KW // 2, 0, GRID_W - NA_KW)
    col_ok = (c[None, :] >= col_start[:, None]) & (c[None, :] < col_start[:, None] + NA_KW)
    dc = np.clip(c[None, :] - c[:, None], -(NA_KW - 1), NA_KW - 1) + NA_KW - 1
    for g in (0, 1, nblk - 1):
        ws = int(np.clip(NA_QROWS * g - NA_QROWS, 0, rows - NA_WROWS))
        r = NA_QROWS * g + np.arange(NA_QROWS)
        r0 = np.clip(r - kh // 2, 0, rows - kh)
        rk = ws + np.arange(NA_WROWS)
        row_ok = (rk[None, :] >= r0[:, None]) & (rk[None, :] < r0[:, None] + kh)
        dr = np.clip(rk[None, :] - r[:, None] + NA_KH - 1, 0, 2 * NA_KH - 2)
        ok = row_ok[:, None, :, None] & col_ok[None, :, None, :]
        dr_i = np.broadcast_to(dr[:, None, :, None], ok.shape).reshape(NA_QB, NA_WB)
        dc_i = np.broadcast_to(dc[None, :, None, :], ok.shape).reshape(NA_QB, NA_WB)
        bias = rpb.astype(F32)[:, dr_i, dc_i]
        tables.append(jnp.where(jnp.asarray(ok.reshape(NA_QB, NA_WB)), bias, NEG_INF))
    return jnp.stack(tables, axis=1)


def _na_attn_kernel(*refs, with_lat, nblk, rows):
    if with_lat:
        q_ref, k_ref, v_ref, kc_ref, vc_ref, bias_ref, o_ref = refs
    else:
        q_ref, kc_ref, vc_ref, o_ref = refs
    lo = _lane_iota() < HEAD_DIM
    q = q_ref[0]
    zero = jnp.zeros_like(q)
    qstack = jnp.concatenate([jnp.where(lo, q, zero), jnp.where(lo, zero, q)], axis=0)
    parts = [_dot_nt(qstack, kc_ref[0])]
    vals = [vc_ref[0]]
    if with_lat:
        g = pl.program_id(2)
        ws = pl.multiple_of(jnp.clip(NA_QROWS * g - NA_QROWS, 0, rows - NA_WROWS) * GRID_W, NA_QB)
        kind = jnp.where(g == 0, 0, jnp.where(g == nblk - 1, 2, 1))
        bias = jnp.concatenate([bias_ref[0, kind], bias_ref[1, kind]], axis=0)
        parts.insert(0, _dot_nt(qstack, k_ref[0, pl.ds(ws, NA_WB), :]) + bias)
        vals.insert(0, v_ref[0, pl.ds(ws, NA_WB), :])
    o = _softmax_pv(parts, vals)
    tq = q.shape[0]
    o_ref[0] = jnp.where(lo, o[:tq], o[tq:]).astype(BF16)


def _na_attn(q, k, v, kc, vc, bias):
    B, L, _ = q.shape
    Lc = kc.shape[1]
    with_lat = k is not None
    npair = NA_HEADS // 2
    if with_lat:
        rows = L // GRID_W
        nblk = rows // NA_QROWS
        grid = (npair, B, nblk)
        qspec = pl.BlockSpec((1, NA_QB, LANES), lambda p, b, g: (b, g, p))
        whole = lambda n: pl.BlockSpec((1, n, LANES), lambda p, b, g: (b, 0, p))
        specs = [qspec, whole(L), whole(L), whole(Lc), whole(Lc),
                 pl.BlockSpec((2, 3, NA_QB, NA_WB), lambda p, b, g: (p, 0, 0, 0))]
        args = [q, k, v, kc, vc, bias]
        kern = functools.partial(_na_attn_kernel, with_lat=True, nblk=nblk, rows=rows)
    else:
        grid = (npair, B)
        qspec = pl.BlockSpec((1, L, LANES), lambda p, b: (b, 0, p))
        specs = [qspec, pl.BlockSpec((1, Lc, LANES), lambda p, b: (b, 0, p)),
                 pl.BlockSpec((1, Lc, LANES), lambda p, b: (b, 0, p))]
        args = [q, kc, vc]
        kern = functools.partial(_na_attn_kernel, with_lat=False, nblk=0, rows=0)
    return pl.pallas_call(
        kern,
        out_shape=jax.ShapeDtypeStruct((B, L, NA_N), BF16),
        grid=grid,
        in_specs=specs,
        out_specs=qspec,
        compiler_params=_cparams(len(grid)),
        name="na_attn" if with_lat else "na_attn_ctx",
    )(*args)


POOL_HALO = SUBLANES


def _pool_kernel(h_ref, hp_ref, hn_ref, mod_ref, g_ref, w_ref, b_ref, sc_ref, o_ref, ext_ref, *, tm, L):
    i = pl.program_id(1)
    nt = L // tm
    g = g_ref[...]
    shift, scale = mod_ref[0, 0:1, :], mod_ref[0, 1:2, :]
    h = h_ref[0]
    a = _modnorm(h, g, shift, scale)
    prev_on = (i > 0).astype(F32)
    next_on = (i < nt - 1).astype(F32)
    ext_ref[0:POOL_HALO, :] = _modnorm(hp_ref[0], g, shift, scale) * prev_on
    ext_ref[POOL_HALO:POOL_HALO + tm, :] = a
    ext_ref[POOL_HALO + tm:2 * POOL_HALO + tm, :] = _modnorm(hn_ref[0], g, shift, scale) * next_on
    t = i * tm + lax.broadcasted_iota(jnp.int32, (tm, 1), 0)
    ys = []
    for gi, w in enumerate(POOL_WINDOWS):
        cols = slice(gi * POOL_DG, (gi + 1) * POOL_DG)
        acc = None
        for k in range(-(w // 2), w - w // 2):
            term = ext_ref[POOL_HALO + k:POOL_HALO + k + tm, cols]
            acc = term if acc is None else acc + term
        first = jnp.clip(t - w // 2, 0, L)
        last = jnp.clip(t - w // 2 + w, 0, L)
        cnt = (last - first).astype(F32)
        p = acc / cnt - a[:, cols]
        ys.append(_dot(p.astype(BF16), w_ref[gi]) + b_ref[gi:gi + 1, :])
    y = jnp.concatenate(ys, axis=-1) * sc_ref[...]
    o_ref[0] = h + mod_ref[0, 2:3, :] * y


def _pool(h, mod, g, w, b, sc):
    B, L, _ = h.shape
    tm = min(L, 256)
    hb = tm // POOL_HALO
    last_halo = L // POOL_HALO - 1
    return pl.pallas_call(
        functools.partial(_pool_kernel, tm=tm, L=L),
        out_shape=jax.ShapeDtypeStruct(h.shape, F32),
        grid=(B, L // tm),
        in_specs=[pl.BlockSpec((1, tm, D), lambda b_, i: (b_, i, 0)),
                  pl.BlockSpec((1, POOL_HALO, D), lambda b_, i: (b_, jnp.maximum(i * hb - 1, 0), 0)),
                  pl.BlockSpec((1, POOL_HALO, D), lambda b_, i: (b_, jnp.minimum((i + 1) * hb, last_halo), 0)),
                  _mod_spec(mod, 0),
                  _const_spec((1, D)),
                  _const_spec((len(POOL_WINDOWS), POOL_DG, POOL_DG)),
                  _const_spec((len(POOL_WINDOWS), POOL_DG)),
                  _const_spec((1, D))],
        out_specs=pl.BlockSpec((1, tm, D), lambda b_, i: (b_, i, 0)),
        scratch_shapes=[pltpu.VMEM((tm + 2 * POOL_HALO, D), F32)],
        compiler_params=_cparams(2),
        name="pool",
    )(h, h, h, mod, g.reshape(1, D), w, b, sc.reshape(1, D))


MLA_A_COLS = 512
MLA_HQ = MLA_HEADS * LANES
MLA_SCALE = MLA_QK ** -0.5


def _mla_proj_kernel(*refs, rope, need_q):
    (h_ref, mod_ref, g_ref, wa_ref, gcq_ref, gckv_ref, wuq_ref, wuk_ref, wuv_ref,
     gq_ref, gk_ref, cos_ref, sin_ref) = refs[:13]
    outs = refs[13:]
    if need_q:
        q_ref, k_ref, v_ref = outs
    else:
        k_ref, v_ref = outs
    a = _modnorm(h_ref[0], g_ref[...], mod_ref[0, 0:1, :], mod_ref[0, 1:2, :]).astype(BF16)
    proj = _dot(a, wa_ref[...])
    if rope:
        cos, sin = cos_ref[...], sin_ref[...]
    if need_q:
        cq = _rms_full(proj[:, :MLA_Q_RANK], gcq_ref[...], MLA_Q_RANK).astype(BF16)
        qh = _dot(cq, wuq_ref[...])
        for hd in range(MLA_HEADS):
            cols = slice(hd * LANES, (hd + 1) * LANES)
            blk = _rms_full(qh[:, cols], gq_ref[...], MLA_QK)
            if rope:
                blk = _rope(blk, cos, sin, MLA_ROPE // 4)
            q_ref[0, :, cols] = (blk * MLA_SCALE).astype(BF16)
    ckv = _rms_full(proj[:, MLA_Q_RANK:MLA_Q_RANK + MLA_KV_RANK], gckv_ref[...], MLA_KV_RANK).astype(BF16)
    kh = _dot(ckv, wuk_ref[...])
    v_ref[0] = _dot(ckv, wuv_ref[...]).astype(BF16)
    kr = pltpu.roll(proj[:, MLA_Q_RANK + MLA_KV_RANK:], MLA_NOPE, 1)
    for hd in range(MLA_HEADS):
        cols = slice(hd * LANES, (hd + 1) * LANES)
        blk = _rms_full(kh[:, cols] + kr, gk_ref[...], MLA_QK)
        if rope:
            blk = _rope(blk, cos, sin, MLA_ROPE // 4)
        k_ref[0, :, cols] = blk.astype(BF16)


def _mla_proj(h, mod, g, wts, cos, sin, rope, need_q):
    B, L, _ = h.shape
    tm = _row_tile(L)
    row = lambda b, i: (b, i, 0)
    out_shape = [jax.ShapeDtypeStruct((B, L, MLA_HQ), BF16), jax.ShapeDtypeStruct((B, L, MLA_HEADS * MLA_V), BF16)]
    out_specs = [pl.BlockSpec((1, tm, MLA_HQ), row), pl.BlockSpec((1, tm, MLA_HEADS * MLA_V), row)]
    if need_q:
        out_shape.insert(0, jax.ShapeDtypeStruct((B, L, MLA_HQ), BF16))
        out_specs.insert(0, pl.BlockSpec((1, tm, MLA_HQ), row))
    return pl.pallas_call(
        functools.partial(_mla_proj_kernel, rope=rope, need_q=need_q),
        out_shape=tuple(out_shape),
        grid=(B, L // tm),
        in_specs=[pl.BlockSpec((1, tm, D), row),
                  _mod_spec(mod, 0),
                  _const_spec((1, D)),
                  _const_spec((D, MLA_A_COLS)),
                  _const_spec((1, MLA_Q_RANK)),
                  _const_spec((1, MLA_KV_RANK)),
                  _const_spec((MLA_Q_RANK, MLA_HQ)),
                  _const_spec((MLA_KV_RANK, MLA_HQ)),
                  _const_spec((MLA_KV_RANK, MLA_HEADS * MLA_V)),
                  _const_spec((1, LANES)),
                  _const_spec((1, LANES)),
                  pl.BlockSpec((tm, LANES), lambda b, i: (i, 0)),
                  pl.BlockSpec((tm, LANES), lambda b, i: (i, 0))],
        out_specs=tuple(out_specs),
        compiler_params=_cparams(2),
        name="mla_proj" if need_q else "mla_proj_kv",
    )(h, mod, g.reshape(1, D), wts["wa"], wts["gcq"], wts["gckv"], wts["wuq"], wts["wuk"], wts["wuv"],
      wts["gq"], wts["gk"], cos[:L], sin[:L])


def _mla_weights(w_a, g_cq, g_ckv, w_uq, w_ukv, g_q, g_k):
    H = MLA_HEADS
    wa = jnp.pad(w_a, ((0, 0), (0, MLA_A_COLS - w_a.shape[1]))).astype(BF16)
    wuq = jnp.pad(w_uq.reshape(MLA_Q_RANK, H, MLA_QK), ((0, 0), (0, 0), (0, LANES - MLA_QK)))
    wukv = w_ukv.reshape(MLA_KV_RANK, H, MLA_NOPE + MLA_V)
    wuk = jnp.pad(wukv[:, :, :MLA_NOPE], ((0, 0), (0, 0), (0, LANES - MLA_NOPE)))
    wuv = wukv[:, :, MLA_NOPE:]
    pad_g = lambda gv: jnp.pad(gv, (0, LANES - MLA_QK)).reshape(1, LANES)
    return dict(wa=wa,
                gcq=g_cq.reshape(1, MLA_Q_RANK), gckv=g_ckv.reshape(1, MLA_KV_RANK),
                wuq=wuq.reshape(MLA_Q_RANK, MLA_HQ).astype(BF16),
                wuk=wuk.reshape(MLA_KV_RANK, MLA_HQ).astype(BF16),
                wuv=wuv.reshape(MLA_KV_RANK, H * MLA_V).astype(BF16),
                gq=pad_g(g_q), gk=pad_g(g_k))


MLA_TQ = 256


def _mla_attn_kernel(q_ref, k_ref, kc_ref, v_ref, vc_ref, o_ref):
    lo = _lane_iota() < MLA_V
    outs = []
    for j in range(2):
        cols = slice(j * LANES, (j + 1) * LANES)
        qj = q_ref[0, :, cols]
        parts = [_dot_nt(qj, k_ref[0, :, cols]), _dot_nt(qj, kc_ref[0, :, cols])]
        outs.append(_softmax_pv(parts, [v_ref[0], vc_ref[0]]))
    o_ref[0] = jnp.where(lo, outs[0], outs[1]).astype(BF16)


def _mla_attn(q, k, kc, v, vc):
    B, L, _ = q.shape
    Lc = kc.shape[1]
    npair = MLA_HEADS // 2
    return pl.pallas_call(
        _mla_attn_kernel,
        out_shape=jax.ShapeDtypeStruct((B, L, MLA_HEADS * MLA_V), BF16),
        grid=(B, npair, L // MLA_TQ),
        in_specs=[pl.BlockSpec((1, MLA_TQ, 2 * LANES), lambda b, p, i: (b, i, p)),
                  pl.BlockSpec((1, L, 2 * LANES), lambda b, p, i: (b, 0, p)),
                  pl.BlockSpec((1, Lc, 2 * LANES), lambda b, p, i: (b, 0, p)),
                  pl.BlockSpec((1, L, LANES), lambda b, p, i: (b, 0, p)),
                  pl.BlockSpec((1, Lc, LANES), lambda b, p, i: (b, 0, p))],
        out_specs=pl.BlockSpec((1, MLA_TQ, LANES), lambda b, p, i: (b, i, p)),
        compiler_params=_cparams(3),
        name="mla_attn",
    )(q, k, kc, v, vc)


def kernel(x, c, ctx, c_ctx, w_ada, b_ada, g_mix, g_ffn, w_gate_up, w_down,
           swa_w_qkv, swa_g_q, swa_g_k, swa_sink, swa_w_o,
           na_w_qkv, na_g_q, na_g_k, na_rpb, na_w_o,
           pool_w, pool_b, pool_scale,
           mla_w_a, mla_g_cq, mla_g_ckv, mla_w_uq, mla_w_ukv, mla_g_q, mla_g_k, mla_w_o):
    B, L, _ = x.shape
    n_rows = -(-(B + 1) // SUBLANES) * SUBLANES
    cvec = jnp.concatenate([c, c_ctx[None, :], jnp.zeros((n_rows - B - 1, D), F32)], axis=0)
    mods = _ada(cvec, w_ada, b_ada).reshape(DEPTH, n_rows, 6, D)

    cos_a, sin_a = _axial_tables(L, HEAD_DIM // 4, 0, HEAD_DIM)
    cos_d, sin_d = _axial_tables(L, MLA_ROPE // 4, MLA_NOPE, LANES)

    h, hc = x, ctx
    for i in range(DEPTH):
        m, j = i % 4, i // 4
        need_ctx = i < DEPTH - 1
        mod = mods[i, :B]
        modc = mods[i, B:B + 1]
        if m == 0:
            w = swa_w_qkv[j].astype(BF16)
            wo = swa_w_o[j].astype(BF16)
            q, kd, vd = _swa_proj(h, mod, g_mix[i], w, swa_g_q[j], swa_g_k[j], cos_a, sin_a, True)
            qc, kcd, vcd = _swa_proj(hc, modc, g_mix[i], w, swa_g_q[j], swa_g_k[j], cos_a, sin_a, False)
            h = _oproj(_swa_attn(q, kd, vd, kcd, vcd, swa_sink[j]), wo, h, mod)
            if need_ctx:
                hc = _oproj(_swa_attn(qc, None, None, kcd, vcd, swa_sink[j]), wo, hc, modc)
        elif m == 1:
            w = na_w_qkv[j].astype(BF16)
            wo = na_w_o[j].astype(BF16)
            q, k, v = _na_proj(h, mod, g_mix[i], w, na_g_q[j], na_g_k[j])
            qc, kc, vc = _na_proj(hc, modc, g_mix[i], w, na_g_q[j], na_g_k[j])
            bias = _na_bias_tables(na_rpb[j], L // GRID_W)
            h = _oproj(_na_attn(q, k, v, kc, vc, bias), wo, h, mod)
            if need_ctx:
                hc = _oproj(_na_attn(qc, None, None, kc, vc, None), wo, hc, modc)
        elif m == 2:
            pw = pool_w[j].astype(BF16)
            h = _pool(h, mod, g_mix[i], pw, pool_b[j], pool_scale[j])
            if need_ctx:
                hc = _pool(hc, modc, g_mix[i], pw, pool_b[j], pool_scale[j])
        else:
            wts = _mla_weights(mla_w_a[j], mla_g_cq[j], mla_g_ckv[j], mla_w_uq[j], mla_w_ukv[j],
                               mla_g_q[j], mla_g_k[j])
            wo = mla_w_o[j].astype(BF16)
            assert not need_ctx, "the latent-attention mixer is only built for the last layer"
            q, k, v = _mla_proj(h, mod, g_mix[i], wts, cos_d, sin_d, True, True)
            kc, vc = _mla_proj(hc, modc, g_mix[i], wts, cos_d, sin_d, False, False)
            h = _oproj(_mla_attn(q, k, kc, v, vc), wo, h, mod)
        wgu = w_gate_up[i].astype(BF16)
        wd = w_down[i].astype(BF16)
        h = _ffn(h, mod, g_ffn[i], wgu, wd)
        if need_ctx:
            hc = _ffn(hc, modc, g_ffn[i], wgu, wd)
    return h
```

```python
import functools

import numpy as np
import jax
import jax.numpy as jnp
from jax import lax
from jax.experimental import pallas as pl
from jax.experimental.pallas import tpu as pltpu

F32 = jnp.float32
BF16 = jnp.bfloat16

D = 1024
DEPTH = 4
GRID_W = 64
EPS = 1e-6
ROPE_THETA = 10000.0
NEG_INF = -1e30
HEAD_DIM = 64
D_FF = 2816
SWA_HEADS = 16
SWA_KV_HEADS = 4
SWA_GROUPS = SWA_HEADS // SWA_KV_HEADS
SWA_WINDOW = 128
NA_HEADS = 16
NA_KH = 8
NA_KW = 16
POOL_WINDOWS = (2, 4, 8, 16)
POOL_DG = D // len(POOL_WINDOWS)
MLA_HEADS = 16
MLA_NOPE = 64
MLA_ROPE = 32
MLA_V = 64
MLA_QK = MLA_NOPE + MLA_ROPE
MLA_Q_RANK = 256
MLA_KV_RANK = 128

LANES = 128
SUBLANES = 8
VMEM_LIMIT = 56 * 1024 * 1024

NA_QROWS = 4
NA_WROWS = 12
NA_QB = NA_QROWS * GRID_W
NA_WB = NA_WROWS * GRID_W


def _cparams(n_axes, carried_last=False):
    sem = ("parallel",) * n_axes
    if carried_last:
        sem = sem[:-1] + ("arbitrary",)
    return pltpu.CompilerParams(dimension_semantics=sem, vmem_limit_bytes=VMEM_LIMIT)


LOG2E = 1.4426950408889634


def _const_spec(shape):
    nd = len(shape)
    return pl.BlockSpec(shape, lambda *_: (0,) * nd, pipeline_mode=pl.Buffered(1))


def _dot(a, b):
    return jnp.dot(a, b, preferred_element_type=F32)


def _dot_nt(a, b):
    return lax.dot_general(a, b, (((1,), (1,)), ((), ())), preferred_element_type=F32)


def _silu(x):
    return x / (1.0 + jnp.exp(-x))


def _lane_iota():
    return lax.broadcasted_iota(jnp.int32, (1, LANES), 1)


def _modnorm(h, g, shift, scale):
    ms = jnp.mean(h * h, axis=-1, keepdims=True)
    y = h * lax.rsqrt(ms + EPS) * g
    return y * (1.0 + scale) + shift


def _rms_full(x, g, n):
    ms = jnp.sum(x * x, axis=-1, keepdims=True) * (1.0 / n)
    return x * lax.rsqrt(ms + EPS) * g


def _rms_head64(y, g):
    lo = _lane_iota() < HEAD_DIM
    x2 = y * y
    sa = jnp.sum(jnp.where(lo, x2, 0.0), axis=-1, keepdims=True)
    sb = jnp.sum(jnp.where(lo, 0.0, x2), axis=-1, keepdims=True)
    ms = jnp.where(lo, sa, sb) * (1.0 / HEAD_DIM)
    return y * lax.rsqrt(ms + EPS) * g


def _rope(y, cos, sin, half):
    first = (_lane_iota() % (2 * half)) < half
    sw = jnp.where(first, pltpu.roll(y, LANES - half, 1), pltpu.roll(y, half, 1))
    return y * cos + sw * sin


def _dup_halves(y):
    lo = _lane_iota() < HEAD_DIM
    yr = pltpu.roll(y, HEAD_DIM, 1)
    return jnp.where(lo, y, yr), jnp.where(lo, yr, y)


def _exp2_parts(parts, sink=None):
    m = None
    for s in parts:
        mi = jnp.max(s, axis=-1, keepdims=True)
        m = mi if m is None else jnp.maximum(m, mi)
    if sink is not None:
        m = jnp.maximum(m, sink)
    es = [jnp.exp2(s - m).astype(BF16) for s in parts]
    return es, (None if sink is None else jnp.exp2(sink - m))


def _pv(es, vals):
    o = None
    for e, v in zip(es, vals):
        oi = _dot(e, v)
        o = oi if o is None else o + oi
    return o


def _norm_lo(o, extra=None):
    den = pltpu.roll(o, HEAD_DIM, 1)
    return o / (den if extra is None else den + extra)


def _norm_hi(o, extra=None):
    return pltpu.roll(o, HEAD_DIM, 1) / (o if extra is None else o + extra)


def _with_ones(v):
    lo = _lane_iota() < HEAD_DIM
    vf = v.astype(F32)
    one = jnp.ones_like(vf)
    return (jnp.where(lo, vf, one).astype(BF16),
            jnp.where(lo, pltpu.roll(vf, HEAD_DIM, 1), one).astype(BF16))


def _ada_kernel(c_ref, w_ref, b_ref, o_ref):
    cond = _silu(c_ref[...]).astype(BF16)
    o_ref[0] = _dot(cond, w_ref[0].astype(BF16)) + b_ref[0]


def _ada(cvec, w_ada, b_ada):
    rows = cvec.shape[0]
    nblk = 6
    return pl.pallas_call(
        _ada_kernel,
        out_shape=jax.ShapeDtypeStruct((DEPTH, rows, 6 * D), F32),
        grid=(DEPTH, nblk),
        in_specs=[pl.BlockSpec((rows, D), lambda i, j: (0, 0)),
                  pl.BlockSpec((1, D, D), lambda i, j: (i, 0, j)),
                  pl.BlockSpec((1, 1, D), lambda i, j: (i, 0, j))],
        out_specs=pl.BlockSpec((1, rows, D), lambda i, j: (i, 0, j)),
        compiler_params=_cparams(2),
        name="ada",
    )(cvec, w_ada, b_ada.reshape(DEPTH, 1, 6 * D))


def _mod_spec(mod, batch_axis):
    if mod.shape[0] == 1:
        return pl.BlockSpec((1, 6, D), lambda *ids: (0, 0, 0))
    return pl.BlockSpec((1, 6, D), lambda *ids: (ids[batch_axis], 0, 0))


def _row_tile(L):
    return min(L, 512)


FFN_CHUNK = 256


def _tail_kernel(*refs, with_oproj):
    if with_oproj:
        o_ref, wo_ref, h_ref, mod_ref, g_ref, wgu_ref, wd_ref, out_ref, act_ref = refs
        h = h_ref[0] + mod_ref[0, 2:3, :] * _dot(o_ref[0], wo_ref[...])
    else:
        h_ref, mod_ref, g_ref, wgu_ref, wd_ref, out_ref, act_ref = refs
        h = h_ref[0]
    a = _modnorm(h, g_ref[...], mod_ref[0, 3:4, :], mod_ref[0, 4:5, :]).astype(BF16)
    for c in range(D_FF // FFN_CHUNK):
        lo = c * FFN_CHUNK
        gate = _dot(a, wgu_ref[:, lo:lo + FFN_CHUNK])
        up = _dot(a, wgu_ref[:, D_FF + lo:D_FF + lo + FFN_CHUNK])
        act_ref[:, lo:lo + FFN_CHUNK] = (_silu(gate) * up).astype(BF16)
    y = _dot(act_ref[...], wd_ref[...])
    out_ref[0] = h + mod_ref[0, 5:6, :] * y


def _tail(o, wo, h, mod, g, wgu, wd):
    B, L, _ = h.shape
    tm = _row_tile(L)
    row = pl.BlockSpec((1, tm, D), lambda b, i: (b, i, 0))
    with_oproj = o is not None
    specs = [row, _mod_spec(mod, 0), _const_spec((1, D)), _const_spec((D, 2 * D_FF)), _const_spec((D_FF, D))]
    args = [h, mod, g.reshape(1, D), wgu, wd]
    if with_oproj:
        specs = [row, _const_spec((D, D))] + specs
        args = [o, wo] + args
    return pl.pallas_call(
        functools.partial(_tail_kernel, with_oproj=with_oproj),
        out_shape=jax.ShapeDtypeStruct(h.shape, F32),
        grid=(B, L // tm),
        in_specs=specs,
        out_specs=row,
        scratch_shapes=[pltpu.VMEM((tm, D_FF), BF16)],
        compiler_params=_cparams(2),
        name="tail" if with_oproj else "ffn",
    )(*args)


def _axial_tables(L, half, lead, width):
    t = np.arange(L)
    freqs = ROPE_THETA ** (-np.arange(half, dtype=np.float64) / half)
    cos = np.ones((L, LANES), np.float64)
    sin = np.zeros((L, LANES), np.float64)
    for base in range(0, LANES, width):
        for axis, pos in enumerate((t // GRID_W, t % GRID_W)):
            ang = pos[:, None].astype(np.float64) * freqs[None, :]
            o = base + lead + axis * 2 * half
            cos[:, o:o + half] = np.cos(ang)
            cos[:, o + half:o + 2 * half] = np.cos(ang)
            sin[:, o:o + half] = -np.sin(ang)
            sin[:, o + half:o + 2 * half] = np.sin(ang)
    return jnp.asarray(cos, F32), jnp.asarray(sin, F32)


SWA_NQ = SWA_HEADS * HEAD_DIM
SWA_NK = SWA_KV_HEADS * HEAD_DIM
SWA_SCALE = HEAD_DIM ** -0.5 * LOG2E


def _swa_proj_kernel(h_ref, mod_ref, g_ref, w_ref, gq_ref, gk_ref, cos_ref, sin_ref,
                     q_ref, k_ref, v_ref, *, rope):
    a = _modnorm(h_ref[0], g_ref[...], mod_ref[0, 0:1, :], mod_ref[0, 1:2, :]).astype(BF16)
    y = _dot(a, w_ref[...])
    if rope:
        cos, sin = cos_ref[...], sin_ref[...]
    for cb in range(SWA_NQ // LANES):
        blk = _rms_head64(y[:, cb * LANES:(cb + 1) * LANES], gq_ref[...])
        if rope:
            blk = _rope(blk, cos, sin, HEAD_DIM // 4)
        q_ref[0, :, cb * LANES:(cb + 1) * LANES] = (blk * SWA_SCALE).astype(BF16)
    for p in range(SWA_NK // LANES):
        blk = _rms_head64(y[:, SWA_NQ + p * LANES:SWA_NQ + (p + 1) * LANES], gk_ref[...])
        if rope:
            blk = _rope(blk, cos, sin, HEAD_DIM // 4)
        ka, kb = _dup_halves(blk)
        k_ref[0, :, (2 * p) * LANES:(2 * p + 1) * LANES] = ka.astype(BF16)
        k_ref[0, :, (2 * p + 1) * LANES:(2 * p + 2) * LANES] = kb.astype(BF16)
        yv = y[:, SWA_NQ + SWA_NK + p * LANES:SWA_NQ + SWA_NK + (p + 1) * LANES]
        lo = _lane_iota() < HEAD_DIM
        v_ref[0, :, (2 * p) * LANES:(2 * p + 1) * LANES] = jnp.where(lo, yv, 1.0).astype(BF16)
        v_ref[0, :, (2 * p + 1) * LANES:(2 * p + 2) * LANES] = (
            jnp.where(lo, pltpu.roll(yv, HEAD_DIM, 1), 1.0).astype(BF16))


def _swa_proj(h, mod, g, w, gq, gk, cos, sin, rope):
    B, L, _ = h.shape
    tm = _row_tile(L)
    nkv = 2 * SWA_NK
    row = lambda b, i: (b, i, 0)
    return pl.pallas_call(
        functools.partial(_swa_proj_kernel, rope=rope),
        out_shape=(jax.ShapeDtypeStruct((B, L, SWA_NQ), BF16),
                   jax.ShapeDtypeStruct((B, L, nkv), BF16),
                   jax.ShapeDtypeStruct((B, L, nkv), BF16)),
        grid=(B, L // tm),
        in_specs=[pl.BlockSpec((1, tm, D), row),
                  _mod_spec(mod, 0),
                  _const_spec((1, D)),
                  _const_spec((D, SWA_NQ + 2 * SWA_NK)),
                  _const_spec((1, LANES)),
                  _const_spec((1, LANES)),
                  pl.BlockSpec((tm, LANES), lambda b, i: (i, 0)),
                  pl.BlockSpec((tm, LANES), lambda b, i: (i, 0))],
        out_specs=(pl.BlockSpec((1, tm, SWA_NQ), row),
                   pl.BlockSpec((1, tm, nkv), row),
                   pl.BlockSpec((1, tm, nkv), row)),
        compiler_params=_cparams(2),
        name="swa_proj",
    )(h, mod, g.reshape(1, D), w, jnp.tile(gq, 2).reshape(1, LANES), jnp.tile(gk, 2).reshape(1, LANES),
      cos[:L], sin[:L])


SWA_QB = 128
SWA_SPAN = SWA_QB + 2 * SWA_WINDOW


def _swa_attn_kernel(*refs, with_window, L):
    if with_window:
        q_ref, kd_ref, vd_ref, kc_ref, vc_ref, sink_ref, o_ref = refs
    else:
        q_ref, kc_ref, vc_ref, sink_ref, o_ref = refs
    lo = _lane_iota() < HEAD_DIM
    rows = SWA_GROUPS * SWA_QB
    if with_window:
        start = pl.program_id(1) * SWA_QB
        w0 = pl.multiple_of(jnp.clip(start - SWA_WINDOW, 0, L - SWA_SPAN), SWA_QB)
        qpos = lax.broadcasted_iota(jnp.int32, (rows, SWA_SPAN), 0) % SWA_QB
        kpos = lax.broadcasted_iota(jnp.int32, (rows, SWA_SPAN), 1)
        valid = jnp.abs(qpos - kpos + (start - w0)) <= SWA_WINDOW
    for hk in range(SWA_KV_HEADS):
        qs = []
        for g in range(SWA_GROUPS):
            cb = hk * 2 + g // 2
            qc = q_ref[0, :, cb * LANES:(cb + 1) * LANES]
            zero = jnp.zeros_like(qc)
            qs.append(jnp.where(lo, qc, zero) if g % 2 == 0 else jnp.where(lo, zero, qc))
        qstack = jnp.concatenate(qs, axis=0)
        sink = jnp.concatenate(
            [jnp.full((SWA_QB, 1), sink_ref[hk * SWA_GROUPS + g] * LOG2E, F32) for g in range(SWA_GROUPS)],
            axis=0)
        cols = slice(hk * LANES, (hk + 1) * LANES)
        parts = [_dot_nt(qstack, kc_ref[0, :, cols])]
        vals = [vc_ref[0, :, cols]]
        if with_window:
            s_w = _dot_nt(qstack, kd_ref[0, pl.ds(w0, SWA_SPAN), cols])
            parts.insert(0, jnp.where(valid, s_w, NEG_INF))
            vals.insert(0, vd_ref[0, pl.ds(w0, SWA_SPAN), cols])
        es, sink_e = _exp2_parts(parts, sink)
        o = _pv(es, vals)
        for j in range(2):
            even = slice((2 * j) * SWA_QB, (2 * j + 1) * SWA_QB)
            odd = slice((2 * j + 1) * SWA_QB, (2 * j + 2) * SWA_QB)
            pair = jnp.where(lo, _norm_lo(o[even], sink_e[even]), _norm_hi(o[odd], sink_e[odd]))
            o_ref[0, :, (hk * 2 + j) * LANES:(hk * 2 + j + 1) * LANES] = pair.astype(BF16)


def _swa_attn(q, kd, vd, kcd, vcd, sink):
    B, L, _ = q.shape
    Lc = kcd.shape[1]
    nkv = 2 * SWA_NK
    with_window = kd is not None
    blk = lambda b, i: (b, i, 0)
    whole = lambda b, i: (b, 0, 0)
    specs = [pl.BlockSpec((1, SWA_QB, SWA_NQ), blk)]
    args = [q]
    if with_window:
        specs += [pl.BlockSpec((1, L, nkv), whole), pl.BlockSpec((1, L, nkv), whole)]
        args += [kd, vd]
    specs += [pl.BlockSpec((1, Lc, nkv), whole), pl.BlockSpec((1, Lc, nkv), whole),
              pl.BlockSpec(memory_space=pltpu.SMEM)]
    args += [kcd, vcd, sink]
    return pl.pallas_call(
        functools.partial(_swa_attn_kernel, with_window=with_window, L=L),
        out_shape=jax.ShapeDtypeStruct((B, L, SWA_NQ), BF16),
        grid=(B, L // SWA_QB),
        in_specs=specs,
        out_specs=pl.BlockSpec((1, SWA_QB, SWA_NQ), blk),
        compiler_params=_cparams(2),
        name="swa_attn" if with_window else "swa_attn_ctx",
    )(*args)


NA_N = NA_HEADS * HEAD_DIM
NA_SCALE = HEAD_DIM ** -0.5 * LOG2E


def _na_proj_kernel(h_ref, mod_ref, g_ref, w_ref, gq_ref, gk_ref, q_ref, k_ref, v_ref):
    a = _modnorm(h_ref[0], g_ref[...], mod_ref[0, 0:1, :], mod_ref[0, 1:2, :]).astype(BF16)
    y = _dot(a, w_ref[...])
    for cb in range(NA_N // LANES):
        cols = slice(cb * LANES, (cb + 1) * LANES)
        q_ref[0, :, cols] = (_rms_head64(y[:, cols], gq_ref[...]) * NA_SCALE).astype(BF16)
        k_ref[0, :, cols] = _rms_head64(y[:, NA_N + cb * LANES:NA_N + (cb + 1) * LANES], gk_ref[...]).astype(BF16)
        v_ref[0, :, cols] = y[:, 2 * NA_N + cb * LANES:2 * NA_N + (cb + 1) * LANES].astype(BF16)


def _na_proj(h, mod, g, w, gq, gk):
    B, L, _ = h.shape
    tm = _row_tile(L)
    row = lambda b, i: (b, i, 0)
    out = jax.ShapeDtypeStruct((B, L, NA_N), BF16)
    return pl.pallas_call(
        _na_proj_kernel,
        out_shape=(out, out, out),
        grid=(B, L // tm),
        in_specs=[pl.BlockSpec((1, tm, D), row),
                  _mod_spec(mod, 0),
                  _const_spec((1, D)),
                  _const_spec((D, 3 * NA_N)),
                  _const_spec((1, LANES)),
                  _const_spec((1, LANES))],
        out_specs=(pl.BlockSpec((1, tm, NA_N), row),) * 3,
        compiler_params=_cparams(2),
        name="na_proj",
    )(h, mod, g.reshape(1, D), w, jnp.tile(gq, 2).reshape(1, LANES), jnp.tile(gk, 2).reshape(1, LANES))


NA_PAIRS = NA_WROWS // 2
NA_OFFSETS = 2 * NA_KH
NA_TILES = 3 * NA_OFFSETS


def _na_bias_tiles(rpb):
    H = rpb.shape[0]
    c = np.arange(GRID_W)
    col_start = np.clip(c - NA_KW // 2, 0, GRID_W - NA_KW)
    col_ok = (c[None, :] >= col_start[:, None]) & (c[None, :] < col_start[:, None] + NA_KW)
    dc = np.clip(c[None, :] - c[:, None], -(NA_KW - 1), NA_KW - 1) + NA_KW - 1
    onehot = (dc[None] == np.arange(2 * NA_KW - 1)[:, None, None]).astype(np.float32)
    e = jnp.einsum("hrd,dqk->hrqk", rpb.astype(F32) * LOG2E, jnp.asarray(onehot),
                   precision=lax.Precision.HIGHEST)
    e = jnp.where(jnp.asarray(col_ok), e, NEG_INF)
    masked = jnp.full((H, 1, GRID_W, GRID_W), NEG_INF, F32)
    ext = jnp.concatenate([masked, e, masked], axis=1)
    first, second = ext[:, :NA_OFFSETS], ext[:, 1:NA_OFFSETS + 1]
    off = jnp.full_like(first, NEG_INF)
    tiles = jnp.stack([jnp.concatenate([first, second], axis=-1),
                       jnp.concatenate([first, off], axis=-1),
                       jnp.concatenate([off, second], axis=-1)], axis=1)
    return tiles.reshape(H, NA_TILES, GRID_W, LANES)


def _na_tile_ids(rows):
    kh = min(NA_KH, rows)
    nblk = rows // NA_QROWS
    ids = np.zeros((3, NA_QROWS, NA_PAIRS), np.int32)
    for kind, g in enumerate((0, 1, nblk - 1)):
        ws = int(np.clip(NA_QROWS * g - NA_QROWS, 0, rows - NA_WROWS))
        for i in range(NA_QROWS):
            r = NA_QROWS * g + i
            r0 = int(np.clip(r - kh // 2, 0, rows - kh))
            for kp in range(NA_PAIRS):
                rk = ws + 2 * kp
                ok1 = r0 <= rk < r0 + kh
                ok2 = r0 <= rk + 1 < r0 + kh
                shifted = rk - r + NA_KH - 1 + 1
                if ok1 and ok2:
                    variant = 0
                elif ok1:
                    variant = 1
                elif ok2:
                    variant = 2
                else:
                    variant, shifted = 1, 0
                assert 0 <= shifted < NA_OFFSETS
                ids[kind, i, kp] = variant * NA_OFFSETS + shifted
    return jnp.asarray(ids.reshape(-1))


def _na_attn_kernel(*refs, with_lat, nblk, rows):
    lo = _lane_iota() < HEAD_DIM
    if with_lat:
        q_ref, k_ref, v_ref, kc_ref, vc_ref, tile_ref, ids_ref, o_ref, v1_ref, vc1_ref = refs
        g = pl.program_id(2)

        @pl.when(g == 0)
        def _():
            v1_ref[0], v1_ref[1] = _with_ones(v_ref[0])
            vc1_ref[0], vc1_ref[1] = _with_ones(vc_ref[0])

        vc_ones = (vc1_ref[0], vc1_ref[1])
    else:
        q_ref, kc_ref, vc_ref, o_ref = refs
        vc_ones = _with_ones(vc_ref[0])
    q = q_ref[0]
    tq = q.shape[0]
    zero = jnp.zeros_like(q)
    qstack = jnp.concatenate([jnp.where(lo, q, zero), jnp.where(lo, zero, q)], axis=0)
    parts = [_dot_nt(qstack, kc_ref[0])]
    if with_lat:
        ws = pl.multiple_of(jnp.clip(NA_QROWS * g - NA_QROWS, 0, rows - NA_WROWS) * GRID_W, NA_QB)
        kind = jnp.where(g == 0, 0, jnp.where(g == nblk - 1, 2, 1))
        base = kind * (NA_QROWS * NA_PAIRS)
        row_blocks = []
        for hh in range(2):
            for i in range(NA_QROWS):
                tiles = [tile_ref[hh, ids_ref[base + i * NA_PAIRS + kp]] for kp in range(NA_PAIRS)]
                row_blocks.append(jnp.concatenate(tiles, axis=1))
        bias = jnp.concatenate(row_blocks, axis=0)
        parts.insert(0, _dot_nt(qstack, k_ref[0, pl.ds(ws, NA_WB), :]) + bias)
    es, _ = _exp2_parts(parts)
    outs = []
    for hh in range(2):
        vals = [vc_ones[hh]]
        if with_lat:
            vals.insert(0, v1_ref[hh, pl.ds(ws, NA_WB), :])
        outs.append(_pv([e[hh * tq:(hh + 1) * tq] for e in es], vals))
    o_ref[0] = jnp.where(lo, _norm_lo(outs[0]), _norm_hi(outs[1])).astype(BF16)


def _na_attn(q, k, v, kc, vc, tiles):
    B, L, _ = q.shape
    Lc = kc.shape[1]
    with_lat = k is not None
    npair = NA_HEADS // 2
    scratch = []
    if with_lat:
        rows = L // GRID_W
        nblk = rows // NA_QROWS
        grid = (npair, B, nblk)
        qspec = pl.BlockSpec((1, NA_QB, LANES), lambda p, b, g: (b, g, p))
        whole = lambda n: pl.BlockSpec((1, n, LANES), lambda p, b, g: (b, 0, p))
        specs = [qspec, whole(L), whole(L), whole(Lc), whole(Lc),
                 pl.BlockSpec((2, NA_TILES, GRID_W, LANES), lambda p, b, g: (p, 0, 0, 0)),
                 pl.BlockSpec(memory_space=pltpu.SMEM)]
        args = [q, k, v, kc, vc, tiles, _na_tile_ids(rows)]
        scratch = [pltpu.VMEM((2, L, LANES), BF16), pltpu.VMEM((2, Lc, LANES), BF16)]
        kern = functools.partial(_na_attn_kernel, with_lat=True, nblk=nblk, rows=rows)
    else:
        grid = (npair, B)
        qspec = pl.BlockSpec((1, L, LANES), lambda p, b: (b, 0, p))
        specs = [qspec, pl.BlockSpec((1, Lc, LANES), lambda p, b: (b, 0, p)),
                 pl.BlockSpec((1, Lc, LANES), lambda p, b: (b, 0, p))]
        args = [q, kc, vc]
        kern = functools.partial(_na_attn_kernel, with_lat=False, nblk=0, rows=0)
    return pl.pallas_call(
        kern,
        out_shape=jax.ShapeDtypeStruct((B, L, NA_N), BF16),
        grid=grid,
        in_specs=specs,
        out_specs=qspec,
        scratch_shapes=scratch,
        compiler_params=_cparams(len(grid), carried_last=with_lat),
        name="na_attn" if with_lat else "na_attn_ctx",
    )(*args)


POOL_HALO = SUBLANES


def _pool_kernel(h_ref, hp_ref, hn_ref, mod_ref, g_ref, w_ref, b_ref, sc_ref, o_ref, ext_ref, *, tm, L):
    i = pl.program_id(1)
    nt = L // tm
    g = g_ref[...]
    shift, scale = mod_ref[0, 0:1, :], mod_ref[0, 1:2, :]
    h = h_ref[0]
    a = _modnorm(h, g, shift, scale)
    prev_on = (i > 0).astype(F32)
    next_on = (i < nt - 1).astype(F32)
    ext_ref[0:POOL_HALO, :] = _modnorm(hp_ref[0], g, shift, scale) * prev_on
    ext_ref[POOL_HALO:POOL_HALO + tm, :] = a
    ext_ref[POOL_HALO + tm:2 * POOL_HALO + tm, :] = _modnorm(hn_ref[0], g, shift, scale) * next_on
    t = i * tm + lax.broadcasted_iota(jnp.int32, (tm, 1), 0)
    ys = []
    for gi, w in enumerate(POOL_WINDOWS):
        cols = slice(gi * POOL_DG, (gi + 1) * POOL_DG)
        acc = None
        for k in range(-(w // 2), w - w // 2):
            term = ext_ref[POOL_HALO + k:POOL_HALO + k + tm, cols]
            acc = term if acc is None else acc + term
        first = jnp.clip(t - w // 2, 0, L)
        last = jnp.clip(t - w // 2 + w, 0, L)
        cnt = (last - first).astype(F32)
        p = acc / cnt - a[:, cols]
        ys.append(_dot(p.astype(BF16), w_ref[gi]) + b_ref[gi:gi + 1, :])
    y = jnp.concatenate(ys, axis=-1) * sc_ref[...]
    o_ref[0] = h + mod_ref[0, 2:3, :] * y


def _pool(h, mod, g, w, b, sc):
    B, L, _ = h.shape
    tm = min(L, 256)
    hb = tm // POOL_HALO
    last_halo = L // POOL_HALO - 1
    return pl.pallas_call(
        functools.partial(_pool_kernel, tm=tm, L=L),
        out_shape=jax.ShapeDtypeStruct(h.shape, F32),
        grid=(B, L // tm),
        in_specs=[pl.BlockSpec((1, tm, D), lambda b_, i: (b_, i, 0)),
                  pl.BlockSpec((1, POOL_HALO, D), lambda b_, i: (b_, jnp.maximum(i * hb - 1, 0), 0)),
                  pl.BlockSpec((1, POOL_HALO, D), lambda b_, i: (b_, jnp.minimum((i + 1) * hb, last_halo), 0)),
                  _mod_spec(mod, 0),
                  _const_spec((1, D)),
                  _const_spec((len(POOL_WINDOWS), POOL_DG, POOL_DG)),
                  _const_spec((len(POOL_WINDOWS), POOL_DG)),
                  _const_spec((1, D))],
        out_specs=pl.BlockSpec((1, tm, D), lambda b_, i: (b_, i, 0)),
        scratch_shapes=[pltpu.VMEM((tm + 2 * POOL_HALO, D), F32)],
        compiler_params=_cparams(2),
        name="pool",
    )(h, h, h, mod, g.reshape(1, D), w, b, sc.reshape(1, D))


MLA_A_COLS = 512
MLA_HQ = MLA_HEADS * LANES
MLA_SCALE = MLA_QK ** -0.5 * LOG2E


def _mla_proj_kernel(*refs, rope, need_q):
    (h_ref, mod_ref, g_ref, wa_ref, gcq_ref, gckv_ref, wuq_ref, wuk_ref, wuv_ref,
     gq_ref, gk_ref, cos_ref, sin_ref) = refs[:13]
    outs = refs[13:]
    if need_q:
        q_ref, k_ref, v_ref = outs
    else:
        k_ref, v_ref = outs
    a = _modnorm(h_ref[0], g_ref[...], mod_ref[0, 0:1, :], mod_ref[0, 1:2, :]).astype(BF16)
    proj = _dot(a, wa_ref[...])
    if rope:
        cos, sin = cos_ref[...], sin_ref[...]
    if need_q:
        cq = _rms_full(proj[:, :MLA_Q_RANK], gcq_ref[...], MLA_Q_RANK).astype(BF16)
        qh = _dot(cq, wuq_ref[...])
        for hd in range(MLA_HEADS):
            cols = slice(hd * LANES, (hd + 1) * LANES)
            blk = _rms_full(qh[:, cols], gq_ref[...], MLA_QK)
            if rope:
                blk = _rope(blk, cos, sin, MLA_ROPE // 4)
            q_ref[0, :, cols] = (blk * MLA_SCALE).astype(BF16)
    ckv = _rms_full(proj[:, MLA_Q_RANK:MLA_Q_RANK + MLA_KV_RANK], gckv_ref[...], MLA_KV_RANK).astype(BF16)
    kh = _dot(ckv, wuk_ref[...])
    v_ref[0] = _dot(ckv, wuv_ref[...]).astype(BF16)
    kr = pltpu.roll(proj[:, MLA_Q_RANK + MLA_KV_RANK:], MLA_NOPE, 1)
    for hd in range(MLA_HEADS):
        cols = slice(hd * LANES, (hd + 1) * LANES)
        blk = _rms_full(kh[:, cols] + kr, gk_ref[...], MLA_QK)
        if rope:
            blk = _rope(blk, cos, sin, MLA_ROPE // 4)
        k_ref[0, :, cols] = blk.astype(BF16)


def _mla_proj(h, mod, g, wts, cos, sin, rope, need_q):
    B, L, _ = h.shape
    tm = _row_tile(L)
    row = lambda b, i: (b, i, 0)
    out_shape = [jax.ShapeDtypeStruct((B, L, MLA_HQ), BF16), jax.ShapeDtypeStruct((B, L, MLA_HEADS * MLA_V), BF16)]
    out_specs = [pl.BlockSpec((1, tm, MLA_HQ), row), pl.BlockSpec((1, tm, MLA_HEADS * MLA_V), row)]
    if need_q:
        out_shape.insert(0, jax.ShapeDtypeStruct((B, L, MLA_HQ), BF16))
        out_specs.insert(0, pl.BlockSpec((1, tm, MLA_HQ), row))
    return pl.pallas_call(
        functools.partial(_mla_proj_kernel, rope=rope, need_q=need_q),
        out_shape=tuple(out_shape),
        grid=(B, L // tm),
        in_specs=[pl.BlockSpec((1, tm, D), row),
                  _mod_spec(mod, 0),
                  _const_spec((1, D)),
                  _const_spec((D, MLA_A_COLS)),
                  _const_spec((1, MLA_Q_RANK)),
                  _const_spec((1, MLA_KV_RANK)),
                  _const_spec((MLA_Q_RANK, MLA_HQ)),
                  _const_spec((MLA_KV_RANK, MLA_HQ)),
                  _const_spec((MLA_KV_RANK, MLA_HEADS * MLA_V)),
                  _const_spec((1, LANES)),
                  _const_spec((1, LANES)),
                  pl.BlockSpec((tm, LANES), lambda b, i: (i, 0)),
                  pl.BlockSpec((tm, LANES), lambda b, i: (i, 0))],
        out_specs=tuple(out_specs),
        compiler_params=_cparams(2),
        name="mla_proj" if need_q else "mla_proj_kv",
    )(h, mod, g.reshape(1, D), wts["wa"], wts["gcq"], wts["gckv"], wts["wuq"], wts["wuk"], wts["wuv"],
      wts["gq"], wts["gk"], cos[:L], sin[:L])


def _mla_weights(w_a, g_cq, g_ckv, w_uq, w_ukv, g_q, g_k):
    H = MLA_HEADS
    wa = jnp.pad(w_a, ((0, 0), (0, MLA_A_COLS - w_a.shape[1]))).astype(BF16)
    wuq = jnp.pad(w_uq.reshape(MLA_Q_RANK, H, MLA_QK), ((0, 0), (0, 0), (0, LANES - MLA_QK)))
    wukv = w_ukv.reshape(MLA_KV_RANK, H, MLA_NOPE + MLA_V)
    wuk = jnp.pad(wukv[:, :, :MLA_NOPE], ((0, 0), (0, 0), (0, LANES - MLA_NOPE)))
    wuv = wukv[:, :, MLA_NOPE:]
    pad_g = lambda gv: jnp.pad(gv, (0, LANES - MLA_QK)).reshape(1, LANES)
    return dict(wa=wa,
                gcq=g_cq.reshape(1, MLA_Q_RANK), gckv=g_ckv.reshape(1, MLA_KV_RANK),
                wuq=wuq.reshape(MLA_Q_RANK, MLA_HQ).astype(BF16),
                wuk=wuk.reshape(MLA_KV_RANK, MLA_HQ).astype(BF16),
                wuv=wuv.reshape(MLA_KV_RANK, H * MLA_V).astype(BF16),
                gq=pad_g(g_q), gk=pad_g(g_k))


MLA_TQ = 512


def _mla_attn_kernel(q_ref, k_ref, kc_ref, v_ref, vc_ref, o_ref, v1_ref, vc1_ref):
    lo = _lane_iota() < MLA_V

    @pl.when(pl.program_id(2) == 0)
    def _():
        v1_ref[0], v1_ref[1] = _with_ones(v_ref[0])
        vc1_ref[0], vc1_ref[1] = _with_ones(vc_ref[0])

    outs = []
    for j in range(2):
        cols = slice(j * LANES, (j + 1) * LANES)
        qj = q_ref[0, :, cols]
        es, _ = _exp2_parts([_dot_nt(qj, k_ref[0, :, cols]), _dot_nt(qj, kc_ref[0, :, cols])])
        outs.append(_pv(es, [v1_ref[j], vc1_ref[j]]))
    o_ref[0] = jnp.where(lo, _norm_lo(outs[0]), _norm_hi(outs[1])).astype(BF16)


def _mla_attn(q, k, kc, v, vc):
    B, L, _ = q.shape
    Lc = kc.shape[1]
    npair = MLA_HEADS // 2
    return pl.pallas_call(
        _mla_attn_kernel,
        out_shape=jax.ShapeDtypeStruct((B, L, MLA_HEADS * MLA_V), BF16),
        grid=(B, npair, L // MLA_TQ),
        in_specs=[pl.BlockSpec((1, MLA_TQ, 2 * LANES), lambda b, p, i: (b, i, p)),
                  pl.BlockSpec((1, L, 2 * LANES), lambda b, p, i: (b, 0, p)),
                  pl.BlockSpec((1, Lc, 2 * LANES), lambda b, p, i: (b, 0, p)),
                  pl.BlockSpec((1, L, LANES), lambda b, p, i: (b, 0, p)),
                  pl.BlockSpec((1, Lc, LANES), lambda b, p, i: (b, 0, p))],
        out_specs=pl.BlockSpec((1, MLA_TQ, LANES), lambda b, p, i: (b, i, p)),
        scratch_shapes=[pltpu.VMEM((2, L, LANES), BF16), pltpu.VMEM((2, Lc, LANES), BF16)],
        compiler_params=_cparams(3, carried_last=True),
        name="mla_attn",
    )(q, k, kc, v, vc)


def kernel(x, c, ctx, c_ctx, w_ada, b_ada, g_mix, g_ffn, w_gate_up, w_down,
           swa_w_qkv, swa_g_q, swa_g_k, swa_sink, swa_w_o,
           na_w_qkv, na_g_q, na_g_k, na_rpb, na_w_o,
           pool_w, pool_b, pool_scale,
           mla_w_a, mla_g_cq, mla_g_ckv, mla_w_uq, mla_w_ukv, mla_g_q, mla_g_k, mla_w_o):
    B, L, _ = x.shape
    n_rows = -(-(B + 1) // SUBLANES) * SUBLANES
    cvec = jnp.concatenate([c, c_ctx[None, :], jnp.zeros((n_rows - B - 1, D), F32)], axis=0)
    mods = _ada(cvec, w_ada, b_ada).reshape(DEPTH, n_rows, 6, D)

    cos_a, sin_a = _axial_tables(L, HEAD_DIM // 4, 0, HEAD_DIM)
    cos_d, sin_d = _axial_tables(L, MLA_ROPE // 4, MLA_NOPE, LANES)

    h, hc = x, ctx
    for i in range(DEPTH):
        m, j = i % 4, i // 4
        need_ctx = i < DEPTH - 1
        mod = mods[i, :B]
        modc = mods[i, B:B + 1]
        o = oc = wo = None
        if m == 0:
            w = swa_w_qkv[j].astype(BF16)
            wo = swa_w_o[j].astype(BF16)
            q, kd, vd = _swa_proj(h, mod, g_mix[i], w, swa_g_q[j], swa_g_k[j], cos_a, sin_a, True)
            qc, kcd, vcd = _swa_proj(hc, modc, g_mix[i], w, swa_g_q[j], swa_g_k[j], cos_a, sin_a, False)
            o = _swa_attn(q, kd, vd, kcd, vcd, swa_sink[j])
            if need_ctx:
                oc = _swa_attn(qc, None, None, kcd, vcd, swa_sink[j])
        elif m == 1:
            w = na_w_qkv[j].astype(BF16)
            wo = na_w_o[j].astype(BF16)
            q, k, v = _na_proj(h, mod, g_mix[i], w, na_g_q[j], na_g_k[j])
            qc, kc, vc = _na_proj(hc, modc, g_mix[i], w, na_g_q[j], na_g_k[j])
            o = _na_attn(q, k, v, kc, vc, _na_bias_tiles(na_rpb[j]))
            if need_ctx:
                oc = _na_attn(qc, None, None, kc, vc, None)
        elif m == 2:
            pw = pool_w[j].astype(BF16)
            h = _pool(h, mod, g_mix[i], pw, pool_b[j], pool_scale[j])
            if need_ctx:
                hc = _pool(hc, modc, g_mix[i], pw, pool_b[j], pool_scale[j])
        else:
            wts = _mla_weights(mla_w_a[j], mla_g_cq[j], mla_g_ckv[j], mla_w_uq[j], mla_w_ukv[j],
                               mla_g_q[j], mla_g_k[j])
            wo = mla_w_o[j].astype(BF16)
            assert not need_ctx, "the latent-attention mixer is only built for the last layer"
            q, k, v = _mla_proj(h, mod, g_mix[i], wts, cos_d, sin_d, True, True)
            kc, vc = _mla_proj(hc, modc, g_mix[i], wts, cos_d, sin_d, False, False)
            o = _mla_attn(q, k, kc, v, vc)
        wgu = w_gate_up[i].astype(BF16)
        wd = w_down[i].astype(BF16)
        h = _tail(o, wo, h, mod, g_ffn[i], wgu, wd)
        if need_ctx:
            hc = _tail(oc, wo if oc is not None else None, hc, modc, g_ffn[i], wgu, wd)
    return h
```

```python
import functools

import numpy as np
import jax
import jax.numpy as jnp
from jax import lax
from jax.experimental import pallas as pl
from jax.experimental.pallas import tpu as pltpu

F32 = jnp.float32
BF16 = jnp.bfloat16

D = 1024
DEPTH = 4
GRID_W = 64
EPS = 1e-6
ROPE_THETA = 10000.0
NEG_INF = -1e30
HEAD_DIM = 64
D_FF = 2816
SWA_HEADS = 16
SWA_KV_HEADS = 4
SWA_GROUPS = SWA_HEADS // SWA_KV_HEADS
SWA_WINDOW = 128
NA_HEADS = 16
NA_KH = 8
NA_KW = 16
POOL_WINDOWS = (2, 4, 8, 16)
POOL_DG = D // len(POOL_WINDOWS)
MLA_HEADS = 16
MLA_NOPE = 64
MLA_ROPE = 32
MLA_V = 64
MLA_QK = MLA_NOPE + MLA_ROPE
MLA_Q_RANK = 256
MLA_KV_RANK = 128

LANES = 128
SUBLANES = 8
VMEM_LIMIT = 56 * 1024 * 1024

NA_QROWS = 4
NA_WROWS = 12
NA_QB = NA_QROWS * GRID_W
NA_WB = NA_WROWS * GRID_W


def _cparams(n_axes, carried_last=False):
    sem = ("parallel",) * n_axes
    if carried_last:
        sem = sem[:-1] + ("arbitrary",)
    return pltpu.CompilerParams(dimension_semantics=sem, vmem_limit_bytes=VMEM_LIMIT)


LOG2E = 1.4426950408889634


def _const_spec(shape):
    nd = len(shape)
    return pl.BlockSpec(shape, lambda *_: (0,) * nd, pipeline_mode=pl.Buffered(1))


def _dot(a, b):
    return jnp.dot(a, b, preferred_element_type=F32)


def _dot_nt(a, b):
    return lax.dot_general(a, b, (((1,), (1,)), ((), ())), preferred_element_type=F32)


def _silu(x):
    return x / (1.0 + jnp.exp(-x))


def _lane_iota():
    return lax.broadcasted_iota(jnp.int32, (1, LANES), 1)


def _modnorm(h, g, shift, scale):
    ms = jnp.mean(h * h, axis=-1, keepdims=True)
    y = h * lax.rsqrt(ms + EPS) * g
    return y * (1.0 + scale) + shift


def _rms_full(x, g, n):
    ms = jnp.sum(x * x, axis=-1, keepdims=True) * (1.0 / n)
    return x * lax.rsqrt(ms + EPS) * g


def _rms_head64(y, g):
    lo = _lane_iota() < HEAD_DIM
    x2 = y * y
    sa = jnp.sum(jnp.where(lo, x2, 0.0), axis=-1, keepdims=True)
    sb = jnp.sum(jnp.where(lo, 0.0, x2), axis=-1, keepdims=True)
    ms = jnp.where(lo, sa, sb) * (1.0 / HEAD_DIM)
    return y * lax.rsqrt(ms + EPS) * g


def _rope(y, cos, sin, half):
    first = (_lane_iota() % (2 * half)) < half
    sw = jnp.where(first, pltpu.roll(y, LANES - half, 1), pltpu.roll(y, half, 1))
    return y * cos + sw * sin


def _dup_halves(y):
    lo = _lane_iota() < HEAD_DIM
    yr = pltpu.roll(y, HEAD_DIM, 1)
    return jnp.where(lo, y, yr), jnp.where(lo, yr, y)


def _exp2_parts(parts, sink=None):
    m = None
    for s in parts:
        mi = jnp.max(s, axis=-1, keepdims=True)
        m = mi if m is None else jnp.maximum(m, mi)
    if sink is not None:
        m = jnp.maximum(m, sink)
    es = [jnp.exp2(s - m).astype(BF16) for s in parts]
    return es, (None if sink is None else jnp.exp2(sink - m))


def _pv(es, vals):
    o = None
    for e, v in zip(es, vals):
        oi = _dot(e, v)
        o = oi if o is None else o + oi
    return o


def _norm_lo(o, extra=None):
    den = pltpu.roll(o, HEAD_DIM, 1)
    return o / (den if extra is None else den + extra)


def _norm_hi(o, extra=None):
    return pltpu.roll(o, HEAD_DIM, 1) / (o if extra is None else o + extra)


def _pipelined(n, scores_fn, finish_fn):
    nxt = scores_fn(0)
    for i in range(n):
        cur = nxt
        if i + 1 < n:
            nxt = scores_fn(i + 1)
        finish_fn(i, cur)


def _with_ones(v):
    lo = _lane_iota() < HEAD_DIM
    vf = v.astype(F32)
    one = jnp.ones_like(vf)
    return (jnp.where(lo, vf, one).astype(BF16),
            jnp.where(lo, pltpu.roll(vf, HEAD_DIM, 1), one).astype(BF16))


def _ada_kernel(c_ref, w_ref, b_ref, o_ref):
    cond = _silu(c_ref[...]).astype(BF16)
    o_ref[0] = _dot(cond, w_ref[0].astype(BF16)) + b_ref[0]


def _ada(cvec, w_ada, b_ada):
    rows = cvec.shape[0]
    nblk = 6
    return pl.pallas_call(
        _ada_kernel,
        out_shape=jax.ShapeDtypeStruct((DEPTH, rows, 6 * D), F32),
        grid=(DEPTH, nblk),
        in_specs=[pl.BlockSpec((rows, D), lambda i, j: (0, 0)),
                  pl.BlockSpec((1, D, D), lambda i, j: (i, 0, j)),
                  pl.BlockSpec((1, 1, D), lambda i, j: (i, 0, j))],
        out_specs=pl.BlockSpec((1, rows, D), lambda i, j: (i, 0, j)),
        compiler_params=_cparams(2),
        name="ada",
    )(cvec, w_ada, b_ada.reshape(DEPTH, 1, 6 * D))


def _mod_spec(mod, batch_axis):
    if mod.shape[0] == 1:
        return pl.BlockSpec((1, 6, D), lambda *ids: (0, 0, 0))
    return pl.BlockSpec((1, 6, D), lambda *ids: (ids[batch_axis], 0, 0))


def _row_tile(L):
    return min(L, 512)


FFN_CHUNK = 256


def _tail_kernel(*refs, with_oproj):
    if with_oproj:
        o_ref, wo_ref, h_ref, mod_ref, g_ref, wgu_ref, wd_ref, out_ref, act_ref = refs
        h = h_ref[0] + mod_ref[0, 2:3, :] * _dot(o_ref[0], wo_ref[...])
    else:
        h_ref, mod_ref, g_ref, wgu_ref, wd_ref, out_ref, act_ref = refs
        h = h_ref[0]
    a = _modnorm(h, g_ref[...], mod_ref[0, 3:4, :], mod_ref[0, 4:5, :]).astype(BF16)
    for c in range(D_FF // FFN_CHUNK):
        lo = c * FFN_CHUNK
        gate = _dot(a, wgu_ref[:, lo:lo + FFN_CHUNK])
        up = _dot(a, wgu_ref[:, D_FF + lo:D_FF + lo + FFN_CHUNK])
        act_ref[:, lo:lo + FFN_CHUNK] = (_silu(gate) * up).astype(BF16)
    y = _dot(act_ref[...], wd_ref[...])
    out_ref[0] = h + mod_ref[0, 5:6, :] * y


def _tail(o, wo, h, mod, g, wgu, wd):
    B, L, _ = h.shape
    tm = _row_tile(L)
    row = pl.BlockSpec((1, tm, D), lambda b, i: (b, i, 0))
    with_oproj = o is not None
    specs = [row, _mod_spec(mod, 0), _const_spec((1, D)), _const_spec((D, 2 * D_FF)), _const_spec((D_FF, D))]
    args = [h, mod, g.reshape(1, D), wgu, wd]
    if with_oproj:
        specs = [row, _const_spec((D, D))] + specs
        args = [o, wo] + args
    return pl.pallas_call(
        functools.partial(_tail_kernel, with_oproj=with_oproj),
        out_shape=jax.ShapeDtypeStruct(h.shape, F32),
        grid=(B, L // tm),
        in_specs=specs,
        out_specs=row,
        scratch_shapes=[pltpu.VMEM((tm, D_FF), BF16)],
        compiler_params=_cparams(2),
        name="tail" if with_oproj else "ffn",
    )(*args)


def _axial_tables(L, half, lead, width):
    t = np.arange(L)
    freqs = ROPE_THETA ** (-np.arange(half, dtype=np.float64) / half)
    cos = np.ones((L, LANES), np.float64)
    sin = np.zeros((L, LANES), np.float64)
    for base in range(0, LANES, width):
        for axis, pos in enumerate((t // GRID_W, t % GRID_W)):
            ang = pos[:, None].astype(np.float64) * freqs[None, :]
            o = base + lead + axis * 2 * half
            cos[:, o:o + half] = np.cos(ang)
            cos[:, o + half:o + 2 * half] = np.cos(ang)
            sin[:, o:o + half] = -np.sin(ang)
            sin[:, o + half:o + 2 * half] = np.sin(ang)
    return jnp.asarray(cos, F32), jnp.asarray(sin, F32)


SWA_NQ = SWA_HEADS * HEAD_DIM
SWA_NK = SWA_KV_HEADS * HEAD_DIM
SWA_SCALE = HEAD_DIM ** -0.5 * LOG2E


def _swa_proj_kernel(h_ref, mod_ref, g_ref, w_ref, gq_ref, gk_ref, cos_ref, sin_ref,
                     q_ref, k_ref, v_ref, *, rope):
    a = _modnorm(h_ref[0], g_ref[...], mod_ref[0, 0:1, :], mod_ref[0, 1:2, :]).astype(BF16)
    y = _dot(a, w_ref[...])
    if rope:
        cos, sin = cos_ref[...], sin_ref[...]
    for cb in range(SWA_NQ // LANES):
        blk = _rms_head64(y[:, cb * LANES:(cb + 1) * LANES], gq_ref[...])
        if rope:
            blk = _rope(blk, cos, sin, HEAD_DIM // 4)
        q_ref[0, :, cb * LANES:(cb + 1) * LANES] = (blk * SWA_SCALE).astype(BF16)
    for p in range(SWA_NK // LANES):
        blk = _rms_head64(y[:, SWA_NQ + p * LANES:SWA_NQ + (p + 1) * LANES], gk_ref[...])
        if rope:
            blk = _rope(blk, cos, sin, HEAD_DIM // 4)
        ka, kb = _dup_halves(blk)
        k_ref[0, :, (2 * p) * LANES:(2 * p + 1) * LANES] = ka.astype(BF16)
        k_ref[0, :, (2 * p + 1) * LANES:(2 * p + 2) * LANES] = kb.astype(BF16)
        yv = y[:, SWA_NQ + SWA_NK + p * LANES:SWA_NQ + SWA_NK + (p + 1) * LANES]
        lo = _lane_iota() < HEAD_DIM
        v_ref[0, :, (2 * p) * LANES:(2 * p + 1) * LANES] = jnp.where(lo, yv, 1.0).astype(BF16)
        v_ref[0, :, (2 * p + 1) * LANES:(2 * p + 2) * LANES] = (
            jnp.where(lo, pltpu.roll(yv, HEAD_DIM, 1), 1.0).astype(BF16))


def _swa_proj(h, mod, g, w, gq, gk, cos, sin, rope):
    B, L, _ = h.shape
    tm = _row_tile(L)
    nkv = 2 * SWA_NK
    row = lambda b, i: (b, i, 0)
    return pl.pallas_call(
        functools.partial(_swa_proj_kernel, rope=rope),
        out_shape=(jax.ShapeDtypeStruct((B, L, SWA_NQ), BF16),
                   jax.ShapeDtypeStruct((B, L, nkv), BF16),
                   jax.ShapeDtypeStruct((B, L, nkv), BF16)),
        grid=(B, L // tm),
        in_specs=[pl.BlockSpec((1, tm, D), row),
                  _mod_spec(mod, 0),
                  _const_spec((1, D)),
                  _const_spec((D, SWA_NQ + 2 * SWA_NK)),
                  _const_spec((1, LANES)),
                  _const_spec((1, LANES)),
                  pl.BlockSpec((tm, LANES), lambda b, i: (i, 0)),
                  pl.BlockSpec((tm, LANES), lambda b, i: (i, 0))],
        out_specs=(pl.BlockSpec((1, tm, SWA_NQ), row),
                   pl.BlockSpec((1, tm, nkv), row),
                   pl.BlockSpec((1, tm, nkv), row)),
        compiler_params=_cparams(2),
        name="swa_proj",
    )(h, mod, g.reshape(1, D), w, jnp.tile(gq, 2).reshape(1, LANES), jnp.tile(gk, 2).reshape(1, LANES),
      cos[:L], sin[:L])


SWA_QB = 256
SWA_SPAN = SWA_QB + 2 * SWA_WINDOW
SWA_STEP_BLOCKS = 2


def _swa_band_masks():
    iq = np.arange(SWA_QB)[:, None]
    ik = np.arange(SWA_SPAN)[None, :]
    masks = [np.where(np.abs(o * SWA_WINDOW + iq - ik) <= SWA_WINDOW, 0.0, NEG_INF) for o in range(3)]
    return jnp.asarray(np.stack(masks), F32)


def _swa_attn_kernel(*refs, with_window, L, nblocks):
    if with_window:
        q_ref, kd_ref, vd_ref, kc_ref, vc_ref, sink_ref, band_ref, o_ref = refs
    else:
        q_ref, kc_ref, vc_ref, sink_ref, o_ref = refs
    lo = _lane_iota() < HEAD_DIM

    def window_start(bi):
        start = (pl.program_id(1) * nblocks + bi) * SWA_QB
        w0 = pl.multiple_of(jnp.clip(start - SWA_WINDOW, 0, L - SWA_SPAN), SWA_WINDOW)
        return start, w0

    def keys_or_values(win_ref, ctx_ref, bi, cols):
        if not with_window:
            return ctx_ref[0, :, cols]
        _, w0 = window_start(bi)
        return jnp.concatenate([win_ref[0, pl.ds(w0, SWA_SPAN), cols], ctx_ref[0, :, cols]], axis=0)

    def scores(c):
        bi, hk = divmod(c, SWA_KV_HEADS)
        qrows = slice(bi * SWA_QB, (bi + 1) * SWA_QB)
        qs = []
        for g in range(SWA_GROUPS):
            cb = hk * 2 + g // 2
            qc = q_ref[0, qrows, cb * LANES:(cb + 1) * LANES]
            zero = jnp.zeros_like(qc)
            qs.append(jnp.where(lo, qc, zero) if g % 2 == 0 else jnp.where(lo, zero, qc))
        qstack = jnp.concatenate(qs, axis=0)
        cols = slice(hk * LANES, (hk + 1) * LANES)
        s = _dot_nt(qstack, keys_or_values(kd_ref if with_window else None, kc_ref, bi, cols))
        if with_window:
            start, w0 = window_start(bi)
            band = band_ref[(start - w0) // SWA_WINDOW]
            band = jnp.concatenate([band] * SWA_GROUPS, axis=0)
            s = jnp.concatenate([s[:, :SWA_SPAN] + band, s[:, SWA_SPAN:]], axis=1)
        return [s]

    def finish(c, parts):
        bi, hk = divmod(c, SWA_KV_HEADS)
        qrows = slice(bi * SWA_QB, (bi + 1) * SWA_QB)
        cols = slice(hk * LANES, (hk + 1) * LANES)
        vals = [keys_or_values(vd_ref if with_window else None, vc_ref, bi, cols)]
        es, sink_e = [], []
        for g in range(SWA_GROUPS):
            head_rows = slice(g * SWA_QB, (g + 1) * SWA_QB)
            eg, sg = _exp2_parts([s[head_rows] for s in parts], sink_ref[hk * SWA_GROUPS + g] * LOG2E)
            es.append(eg[0])
            sink_e.append(sg)
        o = _pv([jnp.concatenate(es, axis=0)], vals)
        for j in range(2):
            even = slice((2 * j) * SWA_QB, (2 * j + 1) * SWA_QB)
            odd = slice((2 * j + 1) * SWA_QB, (2 * j + 2) * SWA_QB)
            pair = jnp.where(lo, _norm_lo(o[even], sink_e[2 * j]), _norm_hi(o[odd], sink_e[2 * j + 1]))
            o_ref[0, qrows, (hk * 2 + j) * LANES:(hk * 2 + j + 1) * LANES] = pair.astype(BF16)

    _pipelined(nblocks * SWA_KV_HEADS, scores, finish)


def _swa_attn(q, kd, vd, kcd, vcd, sink):
    B, L, _ = q.shape
    Lc = kcd.shape[1]
    nkv = 2 * SWA_NK
    with_window = kd is not None
    nblocks = min(SWA_STEP_BLOCKS, L // SWA_QB)
    blk = lambda b, i: (b, i, 0)
    whole = lambda b, i: (b, 0, 0)
    specs = [pl.BlockSpec((1, nblocks * SWA_QB, SWA_NQ), blk)]
    args = [q]
    if with_window:
        specs += [pl.BlockSpec((1, L, nkv), whole), pl.BlockSpec((1, L, nkv), whole)]
        args += [kd, vd]
    specs += [pl.BlockSpec((1, Lc, nkv), whole), pl.BlockSpec((1, Lc, nkv), whole),
              pl.BlockSpec(memory_space=pltpu.SMEM)]
    args += [kcd, vcd, sink]
    if with_window:
        specs.append(_const_spec((3, SWA_QB, SWA_SPAN)))
        args.append(_swa_band_masks())
    return pl.pallas_call(
        functools.partial(_swa_attn_kernel, with_window=with_window, L=L, nblocks=nblocks),
        out_shape=jax.ShapeDtypeStruct((B, L, SWA_NQ), BF16),
        grid=(B, L // (nblocks * SWA_QB)),
        in_specs=specs,
        out_specs=pl.BlockSpec((1, nblocks * SWA_QB, SWA_NQ), blk),
        compiler_params=_cparams(2),
        name="swa_attn" if with_window else "swa_attn_ctx",
    )(*args)


NA_N = NA_HEADS * HEAD_DIM
NA_SCALE = HEAD_DIM ** -0.5 * LOG2E


def _na_proj_kernel(h_ref, mod_ref, g_ref, w_ref, gq_ref, gk_ref, q_ref, k_ref, v_ref):
    a = _modnorm(h_ref[0], g_ref[...], mod_ref[0, 0:1, :], mod_ref[0, 1:2, :]).astype(BF16)
    y = _dot(a, w_ref[...])
    for cb in range(NA_N // LANES):
        cols = slice(cb * LANES, (cb + 1) * LANES)
        q_ref[0, :, cols] = (_rms_head64(y[:, cols], gq_ref[...]) * NA_SCALE).astype(BF16)
        k_ref[0, :, cols] = _rms_head64(y[:, NA_N + cb * LANES:NA_N + (cb + 1) * LANES], gk_ref[...]).astype(BF16)
        v_ref[0, :, cols] = y[:, 2 * NA_N + cb * LANES:2 * NA_N + (cb + 1) * LANES].astype(BF16)


def _na_proj(h, mod, g, w, gq, gk):
    B, L, _ = h.shape
    tm = _row_tile(L)
    row = lambda b, i: (b, i, 0)
    out = jax.ShapeDtypeStruct((B, L, NA_N), BF16)
    return pl.pallas_call(
        _na_proj_kernel,
        out_shape=(out, out, out),
        grid=(B, L // tm),
        in_specs=[pl.BlockSpec((1, tm, D), row),
                  _mod_spec(mod, 0),
                  _const_spec((1, D)),
                  _const_spec((D, 3 * NA_N)),
                  _const_spec((1, LANES)),
                  _const_spec((1, LANES))],
        out_specs=(pl.BlockSpec((1, tm, NA_N), row),) * 3,
        compiler_params=_cparams(2),
        name="na_proj",
    )(h, mod, g.reshape(1, D), w, jnp.tile(gq, 2).reshape(1, LANES), jnp.tile(gk, 2).reshape(1, LANES))


NA_PAIRS = NA_WROWS // 2
NA_OFFSETS = 2 * NA_KH
NA_TILES = 3 * NA_OFFSETS


def _na_bias_tiles(rpb):
    H = rpb.shape[0]
    c = np.arange(GRID_W)
    col_start = np.clip(c - NA_KW // 2, 0, GRID_W - NA_KW)
    col_ok = (c[None, :] >= col_start[:, None]) & (c[None, :] < col_start[:, None] + NA_KW)
    dc = np.clip(c[None, :] - c[:, None], -(NA_KW - 1), NA_KW - 1) + NA_KW - 1
    onehot = (dc[None] == np.arange(2 * NA_KW - 1)[:, None, None]).astype(np.float32)
    e = jnp.einsum("hrd,dqk->hrqk", rpb.astype(F32) * LOG2E, jnp.asarray(onehot),
                   precision=lax.Precision.HIGHEST)
    e = jnp.where(jnp.asarray(col_ok), e, NEG_INF)
    masked = jnp.full((H, 1, GRID_W, GRID_W), NEG_INF, F32)
    ext = jnp.concatenate([masked, e, masked], axis=1)
    first, second = ext[:, :NA_OFFSETS], ext[:, 1:NA_OFFSETS + 1]
    off = jnp.full_like(first, NEG_INF)
    tiles = jnp.stack([jnp.concatenate([first, second], axis=-1),
                       jnp.concatenate([first, off], axis=-1),
                       jnp.concatenate([off, second], axis=-1)], axis=1)
    return tiles.reshape(H, NA_TILES, GRID_W, LANES)


def _na_tile_ids(rows):
    kh = min(NA_KH, rows)
    nblk = rows // NA_QROWS
    ids = np.zeros((nblk, NA_QROWS, NA_PAIRS), np.int32)
    starts = []
    for g in range(nblk):
        ws = int(np.clip(NA_QROWS * g - NA_QROWS, 0, rows - NA_WROWS))
        starts.append(ws)
        for i in range(NA_QROWS):
            r = NA_QROWS * g + i
            r0 = int(np.clip(r - kh // 2, 0, rows - kh))
            for kp in range(NA_PAIRS):
                rk = ws + 2 * kp
                ok1 = r0 <= rk < r0 + kh
                ok2 = r0 <= rk + 1 < r0 + kh
                shifted = rk - r + NA_KH - 1 + 1
                if ok1 and ok2:
                    variant = 0
                elif ok1:
                    variant = 1
                elif ok2:
                    variant = 2
                else:
                    variant, shifted = 1, 0
                assert 0 <= shifted < NA_OFFSETS
                ids[g, i, kp] = variant * NA_OFFSETS + shifted
    return ids, starts


def _na_attn_kernel(*refs, with_lat, npairs, L):
    lo = _lane_iota() < HEAD_DIM
    if with_lat:
        q_ref, k_ref, v_ref, kc_ref, vc_ref, tile_ref, o_ref, v1_ref = refs
        ids, starts = _na_tile_ids(L // GRID_W)
        tq = NA_QB
    else:
        q_ref, kc_ref, vc_ref, o_ref = refs
        tq = L
    nblk = L // tq
    if with_lat:
        v1_ref[0], v1_ref[1] = _with_ones(v_ref[0])
    vc_ones = [_with_ones(vc_ref[0, :, p * LANES:(p + 1) * LANES]) for p in range(npairs)]

    def window(g):
        return slice(starts[g] * GRID_W, starts[g] * GRID_W + NA_WB)

    def scores(c):
        p, g = divmod(c, nblk)
        cols = slice(p * LANES, (p + 1) * LANES)
        q = q_ref[0, g * tq:(g + 1) * tq, cols]
        zero = jnp.zeros_like(q)
        qstack = jnp.concatenate([jnp.where(lo, q, zero), jnp.where(lo, zero, q)], axis=0)
        if not with_lat:
            return [_dot_nt(qstack, kc_ref[0, :, cols])]
        s = _dot_nt(qstack, jnp.concatenate([k_ref[0, window(g), cols], kc_ref[0, :, cols]], axis=0))
        bias = jnp.concatenate(
            [jnp.concatenate([tile_ref[hh, int(ids[g, i, kp])] for kp in range(NA_PAIRS)], axis=1)
             for hh in range(2) for i in range(NA_QROWS)], axis=0)
        return [jnp.concatenate([s[:, :NA_WB] + bias, s[:, NA_WB:]], axis=1)]

    def finish(c, parts):
        p, g = divmod(c, nblk)
        es, _ = _exp2_parts(parts)
        outs = []
        for hh in range(2):
            vals = vc_ones[p][hh]
            if with_lat:
                vals = jnp.concatenate([v1_ref[hh, window(g), :], vals], axis=0)
            outs.append(_pv([e[hh * tq:(hh + 1) * tq] for e in es], [vals]))
        pair = jnp.where(lo, _norm_lo(outs[0]), _norm_hi(outs[1]))
        o_ref[0, g * tq:(g + 1) * tq, p * LANES:(p + 1) * LANES] = pair.astype(BF16)

    _pipelined(npairs * nblk, scores, finish)


def _na_attn(q, k, v, kc, vc, tiles):
    B, L, _ = q.shape
    Lc = kc.shape[1]
    with_lat = k is not None
    scratch = []
    if with_lat:
        npairs = 1
        grid = (NA_HEADS // 2, B)
        blk = lambda n: pl.BlockSpec((1, n, LANES), lambda p, b: (b, 0, p))
        specs = [blk(L), blk(L), blk(L), blk(Lc), blk(Lc),
                 pl.BlockSpec((2, NA_TILES, GRID_W, LANES), lambda p, b: (p, 0, 0, 0))]
        args = [q, k, v, kc, vc, tiles]
        scratch = [pltpu.VMEM((2, L, LANES), BF16)]
        out_spec = blk(L)
    else:
        npairs = NA_HEADS // 2
        grid = (B,)
        blk = lambda n: pl.BlockSpec((1, n, NA_N), lambda b: (b, 0, 0))
        specs = [blk(L), blk(Lc), blk(Lc)]
        args = [q, kc, vc]
        out_spec = blk(L)
    return pl.pallas_call(
        functools.partial(_na_attn_kernel, with_lat=with_lat, npairs=npairs, L=L),
        out_shape=jax.ShapeDtypeStruct((B, L, NA_N), BF16),
        grid=grid,
        in_specs=specs,
        out_specs=out_spec,
        scratch_shapes=scratch,
        compiler_params=_cparams(len(grid)),
        name="na_attn" if with_lat else "na_attn_ctx",
    )(*args)


POOL_HALO = SUBLANES


def _pool_kernel(h_ref, hp_ref, hn_ref, mod_ref, g_ref, w_ref, b_ref, sc_ref, o_ref, ext_ref, *, tm, L):
    i = pl.program_id(1)
    nt = L // tm
    g = g_ref[...]
    shift, scale = mod_ref[0, 0:1, :], mod_ref[0, 1:2, :]
    h = h_ref[0]
    a = _modnorm(h, g, shift, scale)
    prev_on = (i > 0).astype(F32)
    next_on = (i < nt - 1).astype(F32)
    ext_ref[0:POOL_HALO, :] = _modnorm(hp_ref[0], g, shift, scale) * prev_on
    ext_ref[POOL_HALO:POOL_HALO + tm, :] = a
    ext_ref[POOL_HALO + tm:2 * POOL_HALO + tm, :] = _modnorm(hn_ref[0], g, shift, scale) * next_on
    t = i * tm + lax.broadcasted_iota(jnp.int32, (tm, 1), 0)
    ys = []
    for gi, w in enumerate(POOL_WINDOWS):
        cols = slice(gi * POOL_DG, (gi + 1) * POOL_DG)
        acc = None
        for k in range(-(w // 2), w - w // 2):
            term = ext_ref[POOL_HALO + k:POOL_HALO + k + tm, cols]
            acc = term if acc is None else acc + term
        first = jnp.clip(t - w // 2, 0, L)
        last = jnp.clip(t - w // 2 + w, 0, L)
        cnt = (last - first).astype(F32)
        p = acc / cnt - a[:, cols]
        ys.append(_dot(p.astype(BF16), w_ref[gi]) + b_ref[gi:gi + 1, :])
    y = jnp.concatenate(ys, axis=-1) * sc_ref[...]
    o_ref[0] = h + mod_ref[0, 2:3, :] * y


def _pool(h, mod, g, w, b, sc):
    B, L, _ = h.shape
    tm = min(L, 256)
    hb = tm // POOL_HALO
    last_halo = L // POOL_HALO - 1
    return pl.pallas_call(
        functools.partial(_pool_kernel, tm=tm, L=L),
        out_shape=jax.ShapeDtypeStruct(h.shape, F32),
        grid=(B, L // tm),
        in_specs=[pl.BlockSpec((1, tm, D), lambda b_, i: (b_, i, 0)),
                  pl.BlockSpec((1, POOL_HALO, D), lambda b_, i: (b_, jnp.maximum(i * hb - 1, 0), 0)),
                  pl.BlockSpec((1, POOL_HALO, D), lambda b_, i: (b_, jnp.minimum((i + 1) * hb, last_halo), 0)),
                  _mod_spec(mod, 0),
                  _const_spec((1, D)),
                  _const_spec((len(POOL_WINDOWS), POOL_DG, POOL_DG)),
                  _const_spec((len(POOL_WINDOWS), POOL_DG)),
                  _const_spec((1, D))],
        out_specs=pl.BlockSpec((1, tm, D), lambda b_, i: (b_, i, 0)),
        scratch_shapes=[pltpu.VMEM((tm + 2 * POOL_HALO, D), F32)],
        compiler_params=_cparams(2),
        name="pool",
    )(h, h, h, mod, g.reshape(1, D), w, b, sc.reshape(1, D))


MLA_A_COLS = 512
MLA_HQ = MLA_HEADS * LANES
MLA_SCALE = MLA_QK ** -0.5 * LOG2E


def _mla_proj_kernel(*refs, rope, need_q):
    (h_ref, mod_ref, g_ref, wa_ref, gcq_ref, gckv_ref, wuq_ref, wuk_ref, wuv_ref,
     gq_ref, gk_ref, cos_ref, sin_ref) = refs[:13]
    outs = refs[13:]
    if need_q:
        q_ref, k_ref, v_ref = outs
    else:
        k_ref, v_ref = outs
    a = _modnorm(h_ref[0], g_ref[...], mod_ref[0, 0:1, :], mod_ref[0, 1:2, :]).astype(BF16)
    proj = _dot(a, wa_ref[...])
    if rope:
        cos, sin = cos_ref[...], sin_ref[...]
    if need_q:
        cq = _rms_full(proj[:, :MLA_Q_RANK], gcq_ref[...], MLA_Q_RANK).astype(BF16)
        qh = _dot(cq, wuq_ref[...])
        for hd in range(MLA_HEADS):
            cols = slice(hd * LANES, (hd + 1) * LANES)
            blk = _rms_full(qh[:, cols], gq_ref[...], MLA_QK)
            if rope:
                blk = _rope(blk, cos, sin, MLA_ROPE // 4)
            q_ref[0, :, cols] = (blk * MLA_SCALE).astype(BF16)
    ckv = _rms_full(proj[:, MLA_Q_RANK:MLA_Q_RANK + MLA_KV_RANK], gckv_ref[...], MLA_KV_RANK).astype(BF16)
    kh = _dot(ckv, wuk_ref[...])
    v_ref[0] = _dot(ckv, wuv_ref[...]).astype(BF16)
    kr = pltpu.roll(proj[:, MLA_Q_RANK + MLA_KV_RANK:], MLA_NOPE, 1)
    for hd in range(MLA_HEADS):
        cols = slice(hd * LANES, (hd + 1) * LANES)
        blk = _rms_full(kh[:, cols] + kr, gk_ref[...], MLA_QK)
        if rope:
            blk = _rope(blk, cos, sin, MLA_ROPE // 4)
        k_ref[0, :, cols] = blk.astype(BF16)


def _mla_proj(h, mod, g, wts, cos, sin, rope, need_q):
    B, L, _ = h.shape
    tm = _row_tile(L)
    row = lambda b, i: (b, i, 0)
    out_shape = [jax.ShapeDtypeStruct((B, L, MLA_HQ), BF16), jax.ShapeDtypeStruct((B, L, MLA_HEADS * MLA_V), BF16)]
    out_specs = [pl.BlockSpec((1, tm, MLA_HQ), row), pl.BlockSpec((1, tm, MLA_HEADS * MLA_V), row)]
    if need_q:
        out_shape.insert(0, jax.ShapeDtypeStruct((B, L, MLA_HQ), BF16))
        out_specs.insert(0, pl.BlockSpec((1, tm, MLA_HQ), row))
    return pl.pallas_call(
        functools.partial(_mla_proj_kernel, rope=rope, need_q=need_q),
        out_shape=tuple(out_shape),
        grid=(B, L // tm),
        in_specs=[pl.BlockSpec((1, tm, D), row),
                  _mod_spec(mod, 0),
                  _const_spec((1, D)),
                  _const_spec((D, MLA_A_COLS)),
                  _const_spec((1, MLA_Q_RANK)),
                  _const_spec((1, MLA_KV_RANK)),
                  _const_spec((MLA_Q_RANK, MLA_HQ)),
                  _const_spec((MLA_KV_RANK, MLA_HQ)),
                  _const_spec((MLA_KV_RANK, MLA_HEADS * MLA_V)),
                  _const_spec((1, LANES)),
                  _const_spec((1, LANES)),
                  pl.BlockSpec((tm, LANES), lambda b, i: (i, 0)),
                  pl.BlockSpec((tm, LANES), lambda b, i: (i, 0))],
        out_specs=tuple(out_specs),
        compiler_params=_cparams(2),
        name="mla_proj" if need_q else "mla_proj_kv",
    )(h, mod, g.reshape(1, D), wts["wa"], wts["gcq"], wts["gckv"], wts["wuq"], wts["wuk"], wts["wuv"],
      wts["gq"], wts["gk"], cos[:L], sin[:L])


def _mla_weights(w_a, g_cq, g_ckv, w_uq, w_ukv, g_q, g_k):
    H = MLA_HEADS
    wa = jnp.pad(w_a, ((0, 0), (0, MLA_A_COLS - w_a.shape[1]))).astype(BF16)
    wuq = jnp.pad(w_uq.reshape(MLA_Q_RANK, H, MLA_QK), ((0, 0), (0, 0), (0, LANES - MLA_QK)))
    wukv = w_ukv.reshape(MLA_KV_RANK, H, MLA_NOPE + MLA_V)
    wuk = jnp.pad(wukv[:, :, :MLA_NOPE], ((0, 0), (0, 0), (0, LANES - MLA_NOPE)))
    wuv = wukv[:, :, MLA_NOPE:]
    pad_g = lambda gv: jnp.pad(gv, (0, LANES - MLA_QK)).reshape(1, LANES)
    return dict(wa=wa,
                gcq=g_cq.reshape(1, MLA_Q_RANK), gckv=g_ckv.reshape(1, MLA_KV_RANK),
                wuq=wuq.reshape(MLA_Q_RANK, MLA_HQ).astype(BF16),
                wuk=wuk.reshape(MLA_KV_RANK, MLA_HQ).astype(BF16),
                wuv=wuv.reshape(MLA_KV_RANK, H * MLA_V).astype(BF16),
                gq=pad_g(g_q), gk=pad_g(g_k))


MLA_TQ = 512


def _mla_attn_kernel(q_ref, k_ref, kc_ref, v_ref, vc_ref, o_ref, kall_ref, v1_ref):
    lo = _lane_iota() < MLA_V
    L, Lc = k_ref.shape[1], kc_ref.shape[1]
    kall_ref[0:L, :] = k_ref[0]
    kall_ref[L:L + Lc, :] = kc_ref[0]
    v1_ref[0, 0:L, :], v1_ref[1, 0:L, :] = _with_ones(v_ref[0])
    v1_ref[0, L:L + Lc, :], v1_ref[1, L:L + Lc, :] = _with_ones(vc_ref[0])
    first_head = []

    def scores(c):
        qi, j = divmod(c, 2)
        cols = slice(j * LANES, (j + 1) * LANES)
        return [_dot_nt(q_ref[0, qi * MLA_TQ:(qi + 1) * MLA_TQ, cols], kall_ref[:, cols])]

    def finish(c, parts):
        qi, j = divmod(c, 2)
        es, _ = _exp2_parts(parts)
        o = _pv(es, [v1_ref[j]])
        if j == 0:
            first_head.append(_norm_lo(o))
        else:
            pair = jnp.where(lo, first_head.pop(), _norm_hi(o))
            o_ref[0, qi * MLA_TQ:(qi + 1) * MLA_TQ, :] = pair.astype(BF16)

    _pipelined(2 * (q_ref.shape[1] // MLA_TQ), scores, finish)


def _mla_attn(q, k, kc, v, vc):
    B, L, _ = q.shape
    Lc = kc.shape[1]
    npair = MLA_HEADS // 2
    return pl.pallas_call(
        _mla_attn_kernel,
        out_shape=jax.ShapeDtypeStruct((B, L, MLA_HEADS * MLA_V), BF16),
        grid=(B, npair),
        in_specs=[pl.BlockSpec((1, L, 2 * LANES), lambda b, p: (b, 0, p)),
                  pl.BlockSpec((1, L, 2 * LANES), lambda b, p: (b, 0, p)),
                  pl.BlockSpec((1, Lc, 2 * LANES), lambda b, p: (b, 0, p)),
                  pl.BlockSpec((1, L, LANES), lambda b, p: (b, 0, p)),
                  pl.BlockSpec((1, Lc, LANES), lambda b, p: (b, 0, p))],
        out_specs=pl.BlockSpec((1, L, LANES), lambda b, p: (b, 0, p)),
        scratch_shapes=[pltpu.VMEM((L + Lc, 2 * LANES), BF16), pltpu.VMEM((2, L + Lc, LANES), BF16)],
        compiler_params=_cparams(2),
        name="mla_attn",
    )(q, k, kc, v, vc)


def kernel(x, c, ctx, c_ctx, w_ada, b_ada, g_mix, g_ffn, w_gate_up, w_down,
           swa_w_qkv, swa_g_q, swa_g_k, swa_sink, swa_w_o,
           na_w_qkv, na_g_q, na_g_k, na_rpb, na_w_o,
           pool_w, pool_b, pool_scale,
           mla_w_a, mla_g_cq, mla_g_ckv, mla_w_uq, mla_w_ukv, mla_g_q, mla_g_k, mla_w_o):
    B, L, _ = x.shape
    n_rows = -(-(B + 1) // SUBLANES) * SUBLANES
    cvec = jnp.concatenate([c, c_ctx[None, :], jnp.zeros((n_rows - B - 1, D), F32)], axis=0)
    mods = _ada(cvec, w_ada, b_ada).reshape(DEPTH, n_rows, 6, D)

    cos_a, sin_a = _axial_tables(L, HEAD_DIM // 4, 0, HEAD_DIM)
    cos_d, sin_d = _axial_tables(L, MLA_ROPE // 4, MLA_NOPE, LANES)

    h, hc = x, ctx
    for i in range(DEPTH):
        m, j = i % 4, i // 4
        need_ctx = i < DEPTH - 1
        mod = mods[i, :B]
        modc = mods[i, B:B + 1]
        o = oc = wo = None
        if m == 0:
            w = swa_w_qkv[j].astype(BF16)
            wo = swa_w_o[j].astype(BF16)
            q, kd, vd = _swa_proj(h, mod, g_mix[i], w, swa_g_q[j], swa_g_k[j], cos_a, sin_a, True)
            qc, kcd, vcd = _swa_proj(hc, modc, g_mix[i], w, swa_g_q[j], swa_g_k[j], cos_a, sin_a, False)
            o = _swa_attn(q, kd, vd, kcd, vcd, swa_sink[j])
            if need_ctx:
                oc = _swa_attn(qc, None, None, kcd, vcd, swa_sink[j])
        elif m == 1:
            w = na_w_qkv[j].astype(BF16)
            wo = na_w_o[j].astype(BF16)
            q, k, v = _na_proj(h, mod, g_mix[i], w, na_g_q[j], na_g_k[j])
            qc, kc, vc = _na_proj(hc, modc, g_mix[i], w, na_g_q[j], na_g_k[j])
            o = _na_attn(q, k, v, kc, vc, _na_bias_tiles(na_rpb[j]))
            if need_ctx:
                oc = _na_attn(qc, None, None, kc, vc, None)
        elif m == 2:
            pw = pool_w[j].astype(BF16)
            h = _pool(h, mod, g_mix[i], pw, pool_b[j], pool_scale[j])
            if need_ctx:
                hc = _pool(hc, modc, g_mix[i], pw, pool_b[j], pool_scale[j])
        else:
            wts = _mla_weights(mla_w_a[j], mla_g_cq[j], mla_g_ckv[j], mla_w_uq[j], mla_w_ukv[j],
                               mla_g_q[j], mla_g_k[j])
            wo = mla_w_o[j].astype(BF16)
            assert not need_ctx, "the latent-attention mixer is only built for the last layer"
            q, k, v = _mla_proj(h, mod, g_mix[i], wts, cos_d, sin_d, True, True)
            kc, vc = _mla_proj(hc, modc, g_mix[i], wts, cos_d, sin_d, False, False)
            o = _mla_attn(q, k, kc, v, vc)
        wgu = w_gate_up[i].astype(BF16)
        wd = w_down[i].astype(BF16)
        h = _tail(o, wo, h, mod, g_ffn[i], wgu, wd)
        if need_ctx:
            hc = _tail(oc, wo if oc is not None else None, hc, modc, g_ffn[i], wgu, wd)
    return h
```

```python
import functools

import numpy as np
import jax
import jax.numpy as jnp
from jax import lax
from jax.experimental import pallas as pl
from jax.experimental.pallas import tpu as pltpu

F32 = jnp.float32
BF16 = jnp.bfloat16

D = 1024
DEPTH = 4
GRID_W = 64
EPS = 1e-6
ROPE_THETA = 10000.0
NEG_INF = -1e30
HEAD_DIM = 64
D_FF = 2816
SWA_HEADS = 16
SWA_KV_HEADS = 4
SWA_GROUPS = SWA_HEADS // SWA_KV_HEADS
SWA_WINDOW = 128
NA_HEADS = 16
NA_KH = 8
NA_KW = 16
POOL_WINDOWS = (2, 4, 8, 16)
POOL_DG = D // len(POOL_WINDOWS)
MLA_HEADS = 16
MLA_NOPE = 64
MLA_ROPE = 32
MLA_V = 64
MLA_QK = MLA_NOPE + MLA_ROPE
MLA_Q_RANK = 256
MLA_KV_RANK = 128

LANES = 128
SUBLANES = 8
MXU_WIDTH = 256
PROJ_CHUNK = MXU_WIDTH
VMEM_LIMIT = 56 * 1024 * 1024

NA_QROWS = 4
NA_WROWS = 12
NA_QB = NA_QROWS * GRID_W
NA_WB = NA_WROWS * GRID_W


def _cparams(n_axes, carried_last=False):
    sem = ("parallel",) * n_axes
    if carried_last:
        sem = sem[:-1] + ("arbitrary",)
    return pltpu.CompilerParams(dimension_semantics=sem, vmem_limit_bytes=VMEM_LIMIT)


LOG2E = 1.4426950408889634


def _const_spec(shape):
    nd = len(shape)
    return pl.BlockSpec(shape, lambda *_: (0,) * nd, pipeline_mode=pl.Buffered(1))


def _dot(a, b):
    return jnp.dot(a, b, preferred_element_type=F32)


def _dot_nt(a, b):
    return lax.dot_general(a, b, (((1,), (1,)), ((), ())), preferred_element_type=F32)


def _silu(x):
    return x / (1.0 + jnp.exp(-x))


def _lane_iota():
    return lax.broadcasted_iota(jnp.int32, (1, LANES), 1)


def _modnorm(h, g, shift, scale):
    ms = jnp.mean(h * h, axis=-1, keepdims=True)
    y = h * lax.rsqrt(ms + EPS) * g
    return y * (1.0 + scale) + shift


def _rms_full(x, g, n):
    ms = jnp.sum(x * x, axis=-1, keepdims=True) * (1.0 / n)
    return x * lax.rsqrt(ms + EPS) * g


def _rms_head64(y, g):
    lo = _lane_iota() < HEAD_DIM
    x2 = y * y
    sa = jnp.sum(jnp.where(lo, x2, 0.0), axis=-1, keepdims=True)
    sb = jnp.sum(jnp.where(lo, 0.0, x2), axis=-1, keepdims=True)
    ms = jnp.where(lo, sa, sb) * (1.0 / HEAD_DIM)
    return y * lax.rsqrt(ms + EPS) * g


def _rms_head64_mxu(y, g, ones_bd):
    x2 = y * y
    hi = x2.astype(BF16)
    lo = (x2 - hi.astype(F32)).astype(BF16)
    ms = (_dot(hi, ones_bd) + _dot(lo, ones_bd)) * (1.0 / HEAD_DIM)
    return y * lax.rsqrt(ms + EPS) * g


def _rope(y, cos, sin, half):
    first = (_lane_iota() % (2 * half)) < half
    sw = jnp.where(first, pltpu.roll(y, LANES - half, 1), pltpu.roll(y, half, 1))
    return y * cos + sw * sin


def _dup_halves(y):
    lo = _lane_iota() < HEAD_DIM
    yr = pltpu.roll(y, HEAD_DIM, 1)
    return jnp.where(lo, y, yr), jnp.where(lo, yr, y)


def _exp2_parts(parts, sink=None):
    m = None
    for s in parts:
        mi = jnp.max(s, axis=-1, keepdims=True)
        m = mi if m is None else jnp.maximum(m, mi)
    if sink is not None:
        m = jnp.maximum(m, sink)
    es = [jnp.exp2(s - m).astype(BF16) for s in parts]
    return es, (None if sink is None else jnp.exp2(sink - m))


def _pv(es, vals):
    o = None
    for e, v in zip(es, vals):
        oi = _dot(e, v)
        o = oi if o is None else o + oi
    return o


def _norm_lo(o, extra=None):
    den = pltpu.roll(o, HEAD_DIM, 1)
    return o / (den if extra is None else den + extra)


def _norm_hi(o, extra=None):
    return pltpu.roll(o, HEAD_DIM, 1) / (o if extra is None else o + extra)


def _pipelined(n, scores_fn, finish_fn):
    nxt = scores_fn(0)
    for i in range(n):
        cur = nxt
        if i + 1 < n:
            nxt = scores_fn(i + 1)
        finish_fn(i, cur)


def _with_ones(v):
    lo = _lane_iota() < HEAD_DIM
    vf = v.astype(F32)
    one = jnp.ones_like(vf)
    return (jnp.where(lo, vf, one).astype(BF16),
            jnp.where(lo, pltpu.roll(vf, HEAD_DIM, 1), one).astype(BF16))


def _ada_kernel(c_ref, w_ref, b_ref, o_ref):
    cond = _silu(c_ref[...]).astype(BF16)
    o_ref[0] = _dot(cond, w_ref[0].astype(BF16)) + b_ref[0]


def _ada(cvec, w_ada, b_ada):
    rows = cvec.shape[0]
    nblk = 6
    return pl.pallas_call(
        _ada_kernel,
        out_shape=jax.ShapeDtypeStruct((DEPTH, rows, 6 * D), F32),
        grid=(DEPTH, nblk),
        in_specs=[pl.BlockSpec((rows, D), lambda i, j: (0, 0)),
                  pl.BlockSpec((1, D, D), lambda i, j: (i, 0, j)),
                  pl.BlockSpec((1, 1, D), lambda i, j: (i, 0, j))],
        out_specs=pl.BlockSpec((1, rows, D), lambda i, j: (i, 0, j)),
        compiler_params=_cparams(2),
        name="ada",
    )(cvec, w_ada, b_ada.reshape(DEPTH, 1, 6 * D))


def _mod_spec(mod, batch_axis):
    if mod.shape[0] == 1:
        return pl.BlockSpec((1, 6, D), lambda *ids: (0, 0, 0))
    return pl.BlockSpec((1, 6, D), lambda *ids: (ids[batch_axis], 0, 0))


def _row_tile(L):
    return min(L, 512)


FFN_CHUNK = 256


def _tail_kernel(*refs, with_oproj):
    if with_oproj:
        o_ref, wo_ref, h_ref, mod_ref, g_ref, wgu_ref, wd_ref, out_ref, act_ref = refs
        h = h_ref[0] + mod_ref[0, 2:3, :] * _dot(o_ref[0], wo_ref[...])
    else:
        h_ref, mod_ref, g_ref, wgu_ref, wd_ref, out_ref, act_ref = refs
        h = h_ref[0]
    a = _modnorm(h, g_ref[...], mod_ref[0, 3:4, :], mod_ref[0, 4:5, :]).astype(BF16)
    for c in range(D_FF // FFN_CHUNK):
        lo = c * FFN_CHUNK
        gate = _dot(a, wgu_ref[:, lo:lo + FFN_CHUNK])
        up = _dot(a, wgu_ref[:, D_FF + lo:D_FF + lo + FFN_CHUNK])
        act_ref[:, lo:lo + FFN_CHUNK] = (_silu(gate) * up).astype(BF16)
    y = _dot(act_ref[...], wd_ref[...])
    out_ref[0] = h + mod_ref[0, 5:6, :] * y


def _tail(o, wo, h, mod, g, wgu, wd):
    B, L, _ = h.shape
    tm = _row_tile(L)
    row = pl.BlockSpec((1, tm, D), lambda b, i: (b, i, 0))
    with_oproj = o is not None
    specs = [row, _mod_spec(mod, 0), _const_spec((1, D)), _const_spec((D, 2 * D_FF)), _const_spec((D_FF, D))]
    args = [h, mod, g.reshape(1, D), wgu, wd]
    if with_oproj:
        specs = [row, _const_spec((D, D))] + specs
        args = [o, wo] + args
    return pl.pallas_call(
        functools.partial(_tail_kernel, with_oproj=with_oproj),
        out_shape=jax.ShapeDtypeStruct(h.shape, F32),
        grid=(B, L // tm),
        in_specs=specs,
        out_specs=row,
        scratch_shapes=[pltpu.VMEM((tm, D_FF), BF16)],
        compiler_params=_cparams(2),
        name="tail" if with_oproj else "ffn",
    )(*args)


def _axial_tables(L, half, lead, width):
    t = np.arange(L)
    freqs = ROPE_THETA ** (-np.arange(half, dtype=np.float64) / half)
    cos = np.ones((L, LANES), np.float64)
    sin = np.zeros((L, LANES), np.float64)
    for base in range(0, LANES, width):
        for axis, pos in enumerate((t // GRID_W, t % GRID_W)):
            ang = pos[:, None].astype(np.float64) * freqs[None, :]
            o = base + lead + axis * 2 * half
            cos[:, o:o + half] = np.cos(ang)
            cos[:, o + half:o + 2 * half] = np.cos(ang)
            sin[:, o:o + half] = -np.sin(ang)
            sin[:, o + half:o + 2 * half] = np.sin(ang)
    return jnp.asarray(cos, F32), jnp.asarray(sin, F32)


SWA_NQ = SWA_HEADS * HEAD_DIM
SWA_NK = SWA_KV_HEADS * HEAD_DIM
SWA_SCALE = HEAD_DIM ** -0.5 * LOG2E


def _swa_proj_kernel(h_ref, mod_ref, g_ref, w_ref, gq_ref, gk_ref, ones_ref, cos_ref, sin_ref,
                     q_ref, k_ref, v_ref, *, rope):
    a = _modnorm(h_ref[0], g_ref[...], mod_ref[0, 0:1, :], mod_ref[0, 1:2, :]).astype(BF16)
    ones_bd = ones_ref[...]
    lo = _lane_iota() < HEAD_DIM
    n_q = SWA_NQ // PROJ_CHUNK

    def normed(blk, gain):
        blk = _rms_head64_mxu(blk, gain, ones_bd)
        return _rope(blk, cos_ref[...], sin_ref[...], HEAD_DIM // 4) if rope else blk

    def matmul(c):
        return _dot(a, w_ref[:, c * PROJ_CHUNK:(c + 1) * PROJ_CHUNK])

    def epilogue(c, y):
        for half in range(PROJ_CHUNK // LANES):
            blk = y[:, half * LANES:(half + 1) * LANES]
            if c < n_q:
                cb = c * (PROJ_CHUNK // LANES) + half
                q_ref[0, :, cb * LANES:(cb + 1) * LANES] = normed(blk, gq_ref[...]).astype(BF16)
                continue
            out_ref = k_ref if c == n_q else v_ref
            if c == n_q:
                first, second = _dup_halves(normed(blk, gk_ref[...]))
            else:
                first, second = jnp.where(lo, blk, 1.0), jnp.where(lo, pltpu.roll(blk, HEAD_DIM, 1), 1.0)
            out_ref[0, :, (2 * half) * LANES:(2 * half + 1) * LANES] = first.astype(BF16)
            out_ref[0, :, (2 * half + 1) * LANES:(2 * half + 2) * LANES] = second.astype(BF16)

    _pipelined(n_q + 2, matmul, epilogue)


def _head_ones():
    half = np.arange(LANES) // HEAD_DIM
    return jnp.asarray(half[:, None] == half[None, :], BF16)


def _swa_proj(h, mod, g, w, gq, gk, cos, sin, rope):
    B, L, _ = h.shape
    tm = _row_tile(L)
    nkv = 2 * SWA_NK
    row = lambda b, i: (b, i, 0)
    return pl.pallas_call(
        functools.partial(_swa_proj_kernel, rope=rope),
        out_shape=(jax.ShapeDtypeStruct((B, L, SWA_NQ), BF16),
                   jax.ShapeDtypeStruct((B, L, nkv), BF16),
                   jax.ShapeDtypeStruct((B, L, nkv), BF16)),
        grid=(B, L // tm),
        in_specs=[pl.BlockSpec((1, tm, D), row),
                  _mod_spec(mod, 0),
                  _const_spec((1, D)),
                  _const_spec((D, SWA_NQ + 2 * SWA_NK)),
                  _const_spec((1, LANES)),
                  _const_spec((1, LANES)),
                  _const_spec((LANES, LANES)),
                  pl.BlockSpec((tm, LANES), lambda b, i: (i, 0)),
                  pl.BlockSpec((tm, LANES), lambda b, i: (i, 0))],
        out_specs=(pl.BlockSpec((1, tm, SWA_NQ), row),
                   pl.BlockSpec((1, tm, nkv), row),
                   pl.BlockSpec((1, tm, nkv), row)),
        compiler_params=_cparams(2),
        name="swa_proj",
    )(h, mod, g.reshape(1, D), w, (jnp.tile(gq, 2) * SWA_SCALE).reshape(1, LANES),
      jnp.tile(gk, 2).reshape(1, LANES), _head_ones(), cos[:L], sin[:L])


SWA_QB = 256
SWA_SPAN = SWA_QB + 2 * SWA_WINDOW
SWA_STEP_BLOCKS = 2


def _swa_band_masks():
    iq = np.arange(SWA_QB)[:, None]
    ik = np.arange(SWA_SPAN)[None, :]
    masks = [np.where(np.abs(o * SWA_WINDOW + iq - ik) <= SWA_WINDOW, 0.0, NEG_INF) for o in range(3)]
    return jnp.asarray(np.stack(masks), F32)


def _swa_attn_kernel(*refs, with_window, L, nblocks):
    if with_window:
        q_ref, kd_ref, vd_ref, kc_ref, vc_ref, sink_ref, band_ref, o_ref = refs
    else:
        q_ref, kc_ref, vc_ref, sink_ref, o_ref = refs
    lo = _lane_iota() < HEAD_DIM

    def window_start(bi):
        start = (pl.program_id(1) * nblocks + bi) * SWA_QB
        w0 = pl.multiple_of(jnp.clip(start - SWA_WINDOW, 0, L - SWA_SPAN), SWA_WINDOW)
        return start, w0

    def keys_or_values(win_ref, ctx_ref, bi, cols):
        if not with_window:
            return ctx_ref[0, :, cols]
        _, w0 = window_start(bi)
        return jnp.concatenate([win_ref[0, pl.ds(w0, SWA_SPAN), cols], ctx_ref[0, :, cols]], axis=0)

    def scores(c):
        bi, hk = divmod(c, SWA_KV_HEADS)
        qrows = slice(bi * SWA_QB, (bi + 1) * SWA_QB)
        qs = []
        for g in range(SWA_GROUPS):
            cb = hk * 2 + g // 2
            qc = q_ref[0, qrows, cb * LANES:(cb + 1) * LANES]
            zero = jnp.zeros_like(qc)
            qs.append(jnp.where(lo, qc, zero) if g % 2 == 0 else jnp.where(lo, zero, qc))
        qstack = jnp.concatenate(qs, axis=0)
        cols = slice(hk * LANES, (hk + 1) * LANES)
        s = _dot_nt(qstack, keys_or_values(kd_ref if with_window else None, kc_ref, bi, cols))
        if with_window:
            start, w0 = window_start(bi)
            band = band_ref[(start - w0) // SWA_WINDOW]
            band = jnp.concatenate([band] * SWA_GROUPS, axis=0)
            s = jnp.concatenate([s[:, :SWA_SPAN] + band, s[:, SWA_SPAN:]], axis=1)
        return [s]

    def finish(c, parts):
        bi, hk = divmod(c, SWA_KV_HEADS)
        qrows = slice(bi * SWA_QB, (bi + 1) * SWA_QB)
        cols = slice(hk * LANES, (hk + 1) * LANES)
        vals = [keys_or_values(vd_ref if with_window else None, vc_ref, bi, cols)]
        es, sink_e = [], []
        for g in range(SWA_GROUPS):
            head_rows = slice(g * SWA_QB, (g + 1) * SWA_QB)
            eg, sg = _exp2_parts([s[head_rows] for s in parts], sink_ref[hk * SWA_GROUPS + g] * LOG2E)
            es.append(eg[0])
            sink_e.append(sg)
        o = _pv([jnp.concatenate(es, axis=0)], vals)
        for j in range(2):
            even = slice((2 * j) * SWA_QB, (2 * j + 1) * SWA_QB)
            odd = slice((2 * j + 1) * SWA_QB, (2 * j + 2) * SWA_QB)
            pair = jnp.where(lo, _norm_lo(o[even], sink_e[2 * j]), _norm_hi(o[odd], sink_e[2 * j + 1]))
            o_ref[0, qrows, (hk * 2 + j) * LANES:(hk * 2 + j + 1) * LANES] = pair.astype(BF16)

    _pipelined(nblocks * SWA_KV_HEADS, scores, finish)


def _swa_attn(q, kd, vd, kcd, vcd, sink):
    B, L, _ = q.shape
    Lc = kcd.shape[1]
    nkv = 2 * SWA_NK
    with_window = kd is not None
    nblocks = min(SWA_STEP_BLOCKS, L // SWA_QB)
    blk = lambda b, i: (b, i, 0)
    whole = lambda b, i: (b, 0, 0)
    specs = [pl.BlockSpec((1, nblocks * SWA_QB, SWA_NQ), blk)]
    args = [q]
    if with_window:
        specs += [pl.BlockSpec((1, L, nkv), whole), pl.BlockSpec((1, L, nkv), whole)]
        args += [kd, vd]
    specs += [pl.BlockSpec((1, Lc, nkv), whole), pl.BlockSpec((1, Lc, nkv), whole),
              pl.BlockSpec(memory_space=pltpu.SMEM)]
    args += [kcd, vcd, sink]
    if with_window:
        specs.append(_const_spec((3, SWA_QB, SWA_SPAN)))
        args.append(_swa_band_masks())
    return pl.pallas_call(
        functools.partial(_swa_attn_kernel, with_window=with_window, L=L, nblocks=nblocks),
        out_shape=jax.ShapeDtypeStruct((B, L, SWA_NQ), BF16),
        grid=(B, L // (nblocks * SWA_QB)),
        in_specs=specs,
        out_specs=pl.BlockSpec((1, nblocks * SWA_QB, SWA_NQ), blk),
        compiler_params=_cparams(2),
        name="swa_attn" if with_window else "swa_attn_ctx",
    )(*args)


NA_N = NA_HEADS * HEAD_DIM
NA_SCALE = HEAD_DIM ** -0.5 * LOG2E


def _na_proj_kernel(h_ref, mod_ref, g_ref, w_ref, gq_ref, gk_ref, q_ref, k_ref, v_ref):
    a = _modnorm(h_ref[0], g_ref[...], mod_ref[0, 0:1, :], mod_ref[0, 1:2, :]).astype(BF16)
    per_kind = NA_N // PROJ_CHUNK

    def matmul(c):
        return _dot(a, w_ref[:, c * PROJ_CHUNK:(c + 1) * PROJ_CHUNK])

    def epilogue(c, y):
        kind, cc = divmod(c, per_kind)
        for half in range(PROJ_CHUNK // LANES):
            blk = y[:, half * LANES:(half + 1) * LANES]
            cols = slice(cc * PROJ_CHUNK + half * LANES, cc * PROJ_CHUNK + (half + 1) * LANES)
            if kind == 0:
                q_ref[0, :, cols] = _rms_head64(blk, gq_ref[...]).astype(BF16)
            elif kind == 1:
                k_ref[0, :, cols] = _rms_head64(blk, gk_ref[...]).astype(BF16)
            else:
                v_ref[0, :, cols] = blk.astype(BF16)

    _pipelined(3 * per_kind, matmul, epilogue)


def _na_proj(h, mod, g, w, gq, gk):
    B, L, _ = h.shape
    tm = _row_tile(L)
    row = lambda b, i: (b, i, 0)
    out = jax.ShapeDtypeStruct((B, L, NA_N), BF16)
    return pl.pallas_call(
        _na_proj_kernel,
        out_shape=(out, out, out),
        grid=(B, L // tm),
        in_specs=[pl.BlockSpec((1, tm, D), row),
                  _mod_spec(mod, 0),
                  _const_spec((1, D)),
                  _const_spec((D, 3 * NA_N)),
                  _const_spec((1, LANES)),
                  _const_spec((1, LANES))],
        out_specs=(pl.BlockSpec((1, tm, NA_N), row),) * 3,
        compiler_params=_cparams(2),
        name="na_proj",
    )(h, mod, g.reshape(1, D), w, (jnp.tile(gq, 2) * NA_SCALE).reshape(1, LANES),
      jnp.tile(gk, 2).reshape(1, LANES))


NA_PAIRS = NA_WROWS // 2
NA_OFFSETS = 2 * NA_KH
NA_TILES = 3 * NA_OFFSETS


def _na_bias_tiles(rpb):
    H = rpb.shape[0]
    c = np.arange(GRID_W)
    col_start = np.clip(c - NA_KW // 2, 0, GRID_W - NA_KW)
    col_ok = (c[None, :] >= col_start[:, None]) & (c[None, :] < col_start[:, None] + NA_KW)
    dc = np.clip(c[None, :] - c[:, None], -(NA_KW - 1), NA_KW - 1) + NA_KW - 1
    onehot = (dc[None] == np.arange(2 * NA_KW - 1)[:, None, None]).astype(np.float32)
    e = jnp.einsum("hrd,dqk->hrqk", rpb.astype(F32) * LOG2E, jnp.asarray(onehot),
                   precision=lax.Precision.HIGHEST)
    e = jnp.where(jnp.asarray(col_ok), e, NEG_INF)
    masked = jnp.full((H, 1, GRID_W, GRID_W), NEG_INF, F32)
    ext = jnp.concatenate([masked, e, masked], axis=1)
    first, second = ext[:, :NA_OFFSETS], ext[:, 1:NA_OFFSETS + 1]
    off = jnp.full_like(first, NEG_INF)
    tiles = jnp.stack([jnp.concatenate([first, second], axis=-1),
                       jnp.concatenate([first, off], axis=-1),
                       jnp.concatenate([off, second], axis=-1)], axis=1)
    return tiles.reshape(H, NA_TILES, GRID_W, LANES)


def _na_tile_ids(rows):
    kh = min(NA_KH, rows)
    nblk = rows // NA_QROWS
    ids = np.zeros((nblk, NA_QROWS, NA_PAIRS), np.int32)
    starts = []
    for g in range(nblk):
        ws = int(np.clip(NA_QROWS * g - NA_QROWS, 0, rows - NA_WROWS))
        starts.append(ws)
        for i in range(NA_QROWS):
            r = NA_QROWS * g + i
            r0 = int(np.clip(r - kh // 2, 0, rows - kh))
            for kp in range(NA_PAIRS):
                rk = ws + 2 * kp
                ok1 = r0 <= rk < r0 + kh
                ok2 = r0 <= rk + 1 < r0 + kh
                shifted = rk - r + NA_KH - 1 + 1
                if ok1 and ok2:
                    variant = 0
                elif ok1:
                    variant = 1
                elif ok2:
                    variant = 2
                else:
                    variant, shifted = 1, 0
                assert 0 <= shifted < NA_OFFSETS
                ids[g, i, kp] = variant * NA_OFFSETS + shifted
    return ids, starts


def _na_attn_kernel(*refs, with_lat, npairs, L):
    lo = _lane_iota() < HEAD_DIM
    if with_lat:
        q_ref, k_ref, v_ref, kc_ref, vc_ref, tile_ref, o_ref, v1_ref = refs
        ids, starts = _na_tile_ids(L // GRID_W)
        tq = NA_QB
    else:
        q_ref, kc_ref, vc_ref, o_ref = refs
        tq = L
    nblk = L // tq
    if with_lat:
        v1_ref[0], v1_ref[1] = _with_ones(v_ref[0])
    vc_ones = [_with_ones(vc_ref[0, :, p * LANES:(p + 1) * LANES]) for p in range(npairs)]

    def window(g):
        return slice(starts[g] * GRID_W, starts[g] * GRID_W + NA_WB)

    def scores(c):
        p, g = divmod(c, nblk)
        cols = slice(p * LANES, (p + 1) * LANES)
        q = q_ref[0, g * tq:(g + 1) * tq, cols]
        zero = jnp.zeros_like(q)
        qstack = jnp.concatenate([jnp.where(lo, q, zero), jnp.where(lo, zero, q)], axis=0)
        if not with_lat:
            return [_dot_nt(qstack, kc_ref[0, :, cols])]
        s = _dot_nt(qstack, jnp.concatenate([k_ref[0, window(g), cols], kc_ref[0, :, cols]], axis=0))
        bias = jnp.concatenate(
            [jnp.concatenate([tile_ref[hh, int(ids[g, i, kp])] for kp in range(NA_PAIRS)], axis=1)
             for hh in range(2) for i in range(NA_QROWS)], axis=0)
        return [jnp.concatenate([s[:, :NA_WB] + bias, s[:, NA_WB:]], axis=1)]

    def finish(c, parts):
        p, g = divmod(c, nblk)
        es, _ = _exp2_parts(parts)
        outs = []
        for hh in range(2):
            vals = vc_ones[p][hh]
            if with_lat:
                vals = jnp.concatenate([v1_ref[hh, window(g), :], vals], axis=0)
            outs.append(_pv([e[hh * tq:(hh + 1) * tq] for e in es], [vals]))
        pair = jnp.where(lo, _norm_lo(outs[0]), _norm_hi(outs[1]))
        o_ref[0, g * tq:(g + 1) * tq, p * LANES:(p + 1) * LANES] = pair.astype(BF16)

    _pipelined(npairs * nblk, scores, finish)


def _na_attn(q, k, v, kc, vc, tiles):
    B, L, _ = q.shape
    Lc = kc.shape[1]
    with_lat = k is not None
    scratch = []
    if with_lat:
        npairs = 1
        grid = (NA_HEADS // 2, B)
        blk = lambda n: pl.BlockSpec((1, n, LANES), lambda p, b: (b, 0, p))
        specs = [blk(L), blk(L), blk(L), blk(Lc), blk(Lc),
                 pl.BlockSpec((2, NA_TILES, GRID_W, LANES), lambda p, b: (p, 0, 0, 0))]
        args = [q, k, v, kc, vc, tiles]
        scratch = [pltpu.VMEM((2, L, LANES), BF16)]
        out_spec = blk(L)
    else:
        npairs = NA_HEADS // 2
        grid = (B,)
        blk = lambda n: pl.BlockSpec((1, n, NA_N), lambda b: (b, 0, 0))
        specs = [blk(L), blk(Lc), blk(Lc)]
        args = [q, kc, vc]
        out_spec = blk(L)
    return pl.pallas_call(
        functools.partial(_na_attn_kernel, with_lat=with_lat, npairs=npairs, L=L),
        out_shape=jax.ShapeDtypeStruct((B, L, NA_N), BF16),
        grid=grid,
        in_specs=specs,
        out_specs=out_spec,
        scratch_shapes=scratch,
        compiler_params=_cparams(len(grid)),
        name="na_attn" if with_lat else "na_attn_ctx",
    )(*args)


POOL_HALO = SUBLANES


def _pool_kernel(h_ref, hp_ref, hn_ref, mod_ref, g_ref, w_ref, b_ref, sc_ref, o_ref, ext_ref, *, tm, L):
    i = pl.program_id(1)
    nt = L // tm
    g = g_ref[...]
    shift, scale = mod_ref[0, 0:1, :], mod_ref[0, 1:2, :]
    h = h_ref[0]
    a = _modnorm(h, g, shift, scale)
    prev_on = (i > 0).astype(F32)
    next_on = (i < nt - 1).astype(F32)
    ext_ref[0:POOL_HALO, :] = _modnorm(hp_ref[0], g, shift, scale) * prev_on
    ext_ref[POOL_HALO:POOL_HALO + tm, :] = a
    ext_ref[POOL_HALO + tm:2 * POOL_HALO + tm, :] = _modnorm(hn_ref[0], g, shift, scale) * next_on
    t = i * tm + lax.broadcasted_iota(jnp.int32, (tm, 1), 0)
    ys = []
    for gi, w in enumerate(POOL_WINDOWS):
        cols = slice(gi * POOL_DG, (gi + 1) * POOL_DG)
        acc = None
        for k in range(-(w // 2), w - w // 2):
            term = ext_ref[POOL_HALO + k:POOL_HALO + k + tm, cols]
            acc = term if acc is None else acc + term
        first = jnp.clip(t - w // 2, 0, L)
        last = jnp.clip(t - w // 2 + w, 0, L)
        cnt = (last - first).astype(F32)
        p = acc / cnt - a[:, cols]
        ys.append(_dot(p.astype(BF16), w_ref[gi]) + b_ref[gi:gi + 1, :])
    y = jnp.concatenate(ys, axis=-1) * sc_ref[...]
    o_ref[0] = h + mod_ref[0, 2:3, :] * y


def _pool(h, mod, g, w, b, sc):
    B, L, _ = h.shape
    tm = min(L, 256)
    hb = tm // POOL_HALO
    last_halo = L // POOL_HALO - 1
    return pl.pallas_call(
        functools.partial(_pool_kernel, tm=tm, L=L),
        out_shape=jax.ShapeDtypeStruct(h.shape, F32),
        grid=(B, L // tm),
        in_specs=[pl.BlockSpec((1, tm, D), lambda b_, i: (b_, i, 0)),
                  pl.BlockSpec((1, POOL_HALO, D), lambda b_, i: (b_, jnp.maximum(i * hb - 1, 0), 0)),
                  pl.BlockSpec((1, POOL_HALO, D), lambda b_, i: (b_, jnp.minimum((i + 1) * hb, last_halo), 0)),
                  _mod_spec(mod, 0),
                  _const_spec((1, D)),
                  _const_spec((len(POOL_WINDOWS), POOL_DG, POOL_DG)),
                  _const_spec((len(POOL_WINDOWS), POOL_DG)),
                  _const_spec((1, D))],
        out_specs=pl.BlockSpec((1, tm, D), lambda b_, i: (b_, i, 0)),
        scratch_shapes=[pltpu.VMEM((tm + 2 * POOL_HALO, D), F32)],
        compiler_params=_cparams(2),
        name="pool",
    )(h, h, h, mod, g.reshape(1, D), w, b, sc.reshape(1, D))


MLA_A_COLS = 512
MLA_HQ = MLA_HEADS * LANES
MLA_SCALE = MLA_QK ** -0.5 * LOG2E


def _mla_proj_kernel(*refs, rope, need_q):
    (h_ref, mod_ref, g_ref, wa_ref, gcq_ref, gckv_ref, wuq_ref, wuk_ref, wuv_ref,
     gq_ref, gk_ref, ones_ref, cos_ref, sin_ref) = refs[:14]
    outs = refs[14:]
    if need_q:
        q_ref, k_ref, v_ref = outs
    else:
        k_ref, v_ref = outs
    a = _modnorm(h_ref[0], g_ref[...], mod_ref[0, 0:1, :], mod_ref[0, 1:2, :]).astype(BF16)
    proj = _dot(a, wa_ref[...])
    def rotary(blk):
        return blk * cos_ref[...] + pltpu.roll(blk, LANES // 2, 1) * sin_ref[...] if rope else blk

    if need_q:
        cq = _rms_full(proj[:, :MLA_Q_RANK], gcq_ref[...], MLA_Q_RANK).astype(BF16)
        ones = ones_ref[...]
    ckv = _rms_full(proj[:, MLA_Q_RANK:MLA_Q_RANK + MLA_KV_RANK], gckv_ref[...], MLA_KV_RANK).astype(BF16)
    kr = proj[:, MLA_Q_RANK + MLA_KV_RANK:]
    kr_rot = rotary(kr * gk_ref[...])
    kr_ss = jnp.sum(kr * kr, axis=-1, keepdims=True)

    def q_head(blk):
        x2 = blk * blk
        hi = x2.astype(BF16)
        lo = (x2 - hi.astype(F32)).astype(BF16)
        ms = (_dot(hi, ones) + _dot(lo, ones)) * (1.0 / MLA_QK)
        return rotary(blk * lax.rsqrt(ms + EPS) * gq_ref[...])

    def k_head(blk):
        ms = (jnp.sum(blk * blk, axis=-1, keepdims=True) + kr_ss) * (1.0 / MLA_QK)
        return (blk * gk_ref[...] + kr_rot) * lax.rsqrt(ms + EPS)

    heads_per_chunk = PROJ_CHUNK // LANES
    chunks = []
    for j in range(MLA_HEADS // heads_per_chunk):
        chunks += ([("q", j)] if need_q else []) + [("k", j)]
    chunks += [("v", j) for j in range(MLA_HEADS * MLA_V // PROJ_CHUNK)]

    def matmul(c):
        kind, j = chunks[c]
        cols = slice(j * PROJ_CHUNK, (j + 1) * PROJ_CHUNK)
        if kind == "q":
            return _dot(cq, wuq_ref[:, cols])
        return _dot(ckv, wuk_ref[:, cols] if kind == "k" else wuv_ref[:, cols])

    def epilogue(c, y):
        kind, j = chunks[c]
        if kind == "v":
            v_ref[0, :, j * PROJ_CHUNK:(j + 1) * PROJ_CHUNK] = y.astype(BF16)
            return
        for half in range(heads_per_chunk):
            blk = y[:, half * LANES:(half + 1) * LANES]
            cols = slice((j * heads_per_chunk + half) * LANES, (j * heads_per_chunk + half + 1) * LANES)
            if kind == "q":
                q_ref[0, :, cols] = q_head(blk).astype(BF16)
            else:
                k_ref[0, :, cols] = k_head(blk).astype(BF16)

    _pipelined(len(chunks), matmul, epilogue)


def _mla_proj(h, mod, g, wts, cos, sin, rope, need_q):
    B, L, _ = h.shape
    tm = _row_tile(L)
    row = lambda b, i: (b, i, 0)
    out_shape = [jax.ShapeDtypeStruct((B, L, MLA_HQ), BF16), jax.ShapeDtypeStruct((B, L, MLA_HEADS * MLA_V), BF16)]
    out_specs = [pl.BlockSpec((1, tm, MLA_HQ), row), pl.BlockSpec((1, tm, MLA_HEADS * MLA_V), row)]
    if need_q:
        out_shape.insert(0, jax.ShapeDtypeStruct((B, L, MLA_HQ), BF16))
        out_specs.insert(0, pl.BlockSpec((1, tm, MLA_HQ), row))
    return pl.pallas_call(
        functools.partial(_mla_proj_kernel, rope=rope, need_q=need_q),
        out_shape=tuple(out_shape),
        grid=(B, L // tm),
        in_specs=[pl.BlockSpec((1, tm, D), row),
                  _mod_spec(mod, 0),
                  _const_spec((1, D)),
                  _const_spec((D, MLA_A_COLS)),
                  _const_spec((1, MLA_Q_RANK)),
                  _const_spec((1, MLA_KV_RANK)),
                  _const_spec((MLA_Q_RANK, MLA_HQ)),
                  _const_spec((MLA_KV_RANK, MLA_HQ)),
                  _const_spec((MLA_KV_RANK, MLA_HEADS * MLA_V)),
                  _const_spec((1, LANES)),
                  _const_spec((1, LANES)),
                  _const_spec((LANES, LANES)),
                  pl.BlockSpec((tm, LANES), lambda b, i: (i, 0)),
                  pl.BlockSpec((tm, LANES), lambda b, i: (i, 0))],
        out_specs=tuple(out_specs),
        compiler_params=_cparams(2),
        name="mla_proj" if need_q else "mla_proj_kv",
    )(h, mod, g.reshape(1, D), wts["wa"], wts["gcq"], wts["gckv"], wts["wuq"], wts["wuk"], wts["wuv"],
      wts["gq"], wts["gk"], jnp.ones((LANES, LANES), BF16), cos[:L], sin[:L])


def _mla_head_lanes(nope, rope):
    q = MLA_ROPE // 4
    ref = nope if nope is not None else rope
    zeros = lambda n: jnp.zeros(ref.shape[:-1] + (n,), ref.dtype)
    nope_a, nope_b = (zeros(48), zeros(16)) if nope is None else (nope[..., :48], nope[..., 48:])
    if rope is None:
        x1, x2 = zeros(2 * q), zeros(2 * q)
    else:
        x1 = jnp.concatenate([rope[..., 0:q], rope[..., 2 * q:3 * q]], axis=-1)
        x2 = jnp.concatenate([rope[..., q:2 * q], rope[..., 3 * q:4 * q]], axis=-1)
    return jnp.concatenate([nope_a, x1, nope_b, zeros(32), x2], axis=-1)


def _mla_rope_tables(L):
    quarter = MLA_ROPE // 4
    t = np.arange(L)
    freqs = ROPE_THETA ** (-np.arange(quarter, dtype=np.float64) / quarter)
    cos = np.ones((L, LANES), np.float64)
    sin = np.zeros((L, LANES), np.float64)
    for axis, pos in enumerate((t // GRID_W, t % GRID_W)):
        ang = pos[:, None].astype(np.float64) * freqs[None, :]
        x1 = slice(48 + axis * quarter, 48 + (axis + 1) * quarter)
        x2 = slice(112 + axis * quarter, 112 + (axis + 1) * quarter)
        cos[:, x1] = np.cos(ang)
        cos[:, x2] = np.cos(ang)
        sin[:, x1] = -np.sin(ang)
        sin[:, x2] = np.sin(ang)
    return jnp.asarray(cos, F32), jnp.asarray(sin, F32)


def _mla_weights(w_a, g_cq, g_ckv, w_uq, w_ukv, g_q, g_k):
    H = MLA_HEADS
    spread = lambda t: _mla_head_lanes(t[..., :MLA_NOPE], t[..., MLA_NOPE:])
    n_lin = MLA_Q_RANK + MLA_KV_RANK
    wa = jnp.concatenate([w_a[:, :n_lin], _mla_head_lanes(None, w_a[:, n_lin:])], axis=1)
    wukv = w_ukv.reshape(MLA_KV_RANK, H, MLA_NOPE + MLA_V)
    wuk = _mla_head_lanes(wukv[:, :, :MLA_NOPE], None)
    return dict(wa=wa.astype(BF16),
                gcq=g_cq.reshape(1, MLA_Q_RANK), gckv=g_ckv.reshape(1, MLA_KV_RANK),
                wuq=spread(w_uq.reshape(MLA_Q_RANK, H, MLA_QK)).reshape(MLA_Q_RANK, MLA_HQ).astype(BF16),
                wuk=wuk.reshape(MLA_KV_RANK, MLA_HQ).astype(BF16),
                wuv=wukv[:, :, MLA_NOPE:].reshape(MLA_KV_RANK, H * MLA_V).astype(BF16),
                gq=(spread(g_q) * MLA_SCALE).reshape(1, LANES), gk=spread(g_k).reshape(1, LANES))


MLA_TQ = 512


def _mla_attn_kernel(q_ref, k_ref, kc_ref, v_ref, vc_ref, o_ref, kall_ref, v1_ref):
    lo = _lane_iota() < MLA_V
    L, Lc = k_ref.shape[1], kc_ref.shape[1]
    kall_ref[0:L, :] = k_ref[0]
    kall_ref[L:L + Lc, :] = kc_ref[0]
    v1_ref[0, 0:L, :], v1_ref[1, 0:L, :] = _with_ones(v_ref[0])
    v1_ref[0, L:L + Lc, :], v1_ref[1, L:L + Lc, :] = _with_ones(vc_ref[0])
    first_head = []

    def scores(c):
        qi, j = divmod(c, 2)
        cols = slice(j * LANES, (j + 1) * LANES)
        return [_dot_nt(q_ref[0, qi * MLA_TQ:(qi + 1) * MLA_TQ, cols], kall_ref[:, cols])]

    def finish(c, parts):
        qi, j = divmod(c, 2)
        es, _ = _exp2_parts(parts)
        o = _pv(es, [v1_ref[j]])
        if j == 0:
            first_head.append(_norm_lo(o))
        else:
            pair = jnp.where(lo, first_head.pop(), _norm_hi(o))
            o_ref[0, qi * MLA_TQ:(qi + 1) * MLA_TQ, :] = pair.astype(BF16)

    _pipelined(2 * (q_ref.shape[1] // MLA_TQ), scores, finish)


def _mla_attn(q, k, kc, v, vc):
    B, L, _ = q.shape
    Lc = kc.shape[1]
    npair = MLA_HEADS // 2
    return pl.pallas_call(
        _mla_attn_kernel,
        out_shape=jax.ShapeDtypeStruct((B, L, MLA_HEADS * MLA_V), BF16),
        grid=(B, npair),
        in_specs=[pl.BlockSpec((1, L, 2 * LANES), lambda b, p: (b, 0, p)),
                  pl.BlockSpec((1, L, 2 * LANES), lambda b, p: (b, 0, p)),
                  pl.BlockSpec((1, Lc, 2 * LANES), lambda b, p: (b, 0, p)),
                  pl.BlockSpec((1, L, LANES), lambda b, p: (b, 0, p)),
                  pl.BlockSpec((1, Lc, LANES), lambda b, p: (b, 0, p))],
        out_specs=pl.BlockSpec((1, L, LANES), lambda b, p: (b, 0, p)),
        scratch_shapes=[pltpu.VMEM((L + Lc, 2 * LANES), BF16), pltpu.VMEM((2, L + Lc, LANES), BF16)],
        compiler_params=_cparams(2),
        name="mla_attn",
    )(q, k, kc, v, vc)


def kernel(x, c, ctx, c_ctx, w_ada, b_ada, g_mix, g_ffn, w_gate_up, w_down,
           swa_w_qkv, swa_g_q, swa_g_k, swa_sink, swa_w_o,
           na_w_qkv, na_g_q, na_g_k, na_rpb, na_w_o,
           pool_w, pool_b, pool_scale,
           mla_w_a, mla_g_cq, mla_g_ckv, mla_w_uq, mla_w_ukv, mla_g_q, mla_g_k, mla_w_o):
    B, L, _ = x.shape
    n_rows = -(-(B + 1) // SUBLANES) * SUBLANES
    cvec = jnp.concatenate([c, c_ctx[None, :], jnp.zeros((n_rows - B - 1, D), F32)], axis=0)
    mods = _ada(cvec, w_ada, b_ada).reshape(DEPTH, n_rows, 6, D)

    cos_a, sin_a = _axial_tables(L, HEAD_DIM // 4, 0, HEAD_DIM)
    cos_d, sin_d = _mla_rope_tables(L)

    h, hc = x, ctx
    for i in range(DEPTH):
        m, j = i % 4, i // 4
        need_ctx = i < DEPTH - 1
        mod = mods[i, :B]
        modc = mods[i, B:B + 1]
        o = oc = wo = None
        if m == 0:
            w = swa_w_qkv[j].astype(BF16)
            wo = swa_w_o[j].astype(BF16)
            q, kd, vd = _swa_proj(h, mod, g_mix[i], w, swa_g_q[j], swa_g_k[j], cos_a, sin_a, True)
            qc, kcd, vcd = _swa_proj(hc, modc, g_mix[i], w, swa_g_q[j], swa_g_k[j], cos_a, sin_a, False)
            o = _swa_attn(q, kd, vd, kcd, vcd, swa_sink[j])
            if need_ctx:
                oc = _swa_attn(qc, None, None, kcd, vcd, swa_sink[j])
        elif m == 1:
            w = na_w_qkv[j].astype(BF16)
            wo = na_w_o[j].astype(BF16)
            q, k, v = _na_proj(h, mod, g_mix[i], w, na_g_q[j], na_g_k[j])
            qc, kc, vc = _na_proj(hc, modc, g_mix[i], w, na_g_q[j], na_g_k[j])
            o = _na_attn(q, k, v, kc, vc, _na_bias_tiles(na_rpb[j]))
            if need_ctx:
                oc = _na_attn(qc, None, None, kc, vc, None)
        elif m == 2:
            pw = pool_w[j].astype(BF16)
            h = _pool(h, mod, g_mix[i], pw, pool_b[j], pool_scale[j])
            if need_ctx:
                hc = _pool(hc, modc, g_mix[i], pw, pool_b[j], pool_scale[j])
        else:
            wts = _mla_weights(mla_w_a[j], mla_g_cq[j], mla_g_ckv[j], mla_w_uq[j], mla_w_ukv[j],
                               mla_g_q[j], mla_g_k[j])
            wo = mla_w_o[j].astype(BF16)
            assert not need_ctx, "the latent-attention mixer is only built for the last layer"
            q, k, v = _mla_proj(h, mod, g_mix[i], wts, cos_d, sin_d, True, True)
            kc, vc = _mla_proj(hc, modc, g_mix[i], wts, cos_d, sin_d, False, False)
            o = _mla_attn(q, k, kc, v, vc)
        wgu = w_gate_up[i].astype(BF16)
        wd = w_down[i].astype(BF16)
        h = _tail(o, wo, h, mod, g_ffn[i], wgu, wd)
        if need_ctx:
            hc = _tail(oc, wo if oc is not None else None, hc, modc, g_ffn[i], wgu, wd)
    return h
```

```python
import functools

import numpy as np
import jax
import jax.numpy as jnp
from jax import lax
from jax.experimental import pallas as pl
from jax.experimental.pallas import tpu as pltpu

F32 = jnp.float32
BF16 = jnp.bfloat16

D = 1024
DEPTH = 4
GRID_W = 64
EPS = 1e-6
ROPE_THETA = 10000.0
NEG_INF = -1e30
HEAD_DIM = 64
D_FF = 2816
SWA_HEADS = 16
SWA_KV_HEADS = 4
SWA_GROUPS = SWA_HEADS // SWA_KV_HEADS
SWA_WINDOW = 128
NA_HEADS = 16
NA_KH = 8
NA_KW = 16
POOL_WINDOWS = (2, 4, 8, 16)
POOL_DG = D // len(POOL_WINDOWS)
MLA_HEADS = 16
MLA_NOPE = 64
MLA_ROPE = 32
MLA_V = 64
MLA_QK = MLA_NOPE + MLA_ROPE
MLA_Q_RANK = 256
MLA_KV_RANK = 128

LANES = 128
SUBLANES = 8
MXU_WIDTH = 256
PROJ_CHUNK = MXU_WIDTH
VMEM_LIMIT = 56 * 1024 * 1024

NA_QROWS = 4
NA_WROWS = 12
NA_QB = NA_QROWS * GRID_W
NA_WB = NA_WROWS * GRID_W


def _cparams(n_axes, carried_last=False):
    sem = ("parallel",) * n_axes
    if carried_last:
        sem = sem[:-1] + ("arbitrary",)
    return pltpu.CompilerParams(dimension_semantics=sem, vmem_limit_bytes=VMEM_LIMIT)


LOG2E = 1.4426950408889634


def _const_spec(shape):
    nd = len(shape)
    return pl.BlockSpec(shape, lambda *_: (0,) * nd, pipeline_mode=pl.Buffered(1))


def _dot(a, b):
    return jnp.dot(a, b, preferred_element_type=F32)


def _dot_nt(a, b):
    return lax.dot_general(a, b, (((1,), (1,)), ((), ())), preferred_element_type=F32)


def _silu(x):
    return x / (1.0 + jnp.exp(-x))


def _lane_iota():
    return lax.broadcasted_iota(jnp.int32, (1, LANES), 1)


def _modnorm(h, g, shift, scale):
    ms = jnp.mean(h * h, axis=-1, keepdims=True)
    y = h * lax.rsqrt(ms + EPS) * g
    return y * (1.0 + scale) + shift


def _rms_full(x, g, n):
    ms = jnp.sum(x * x, axis=-1, keepdims=True) * (1.0 / n)
    return x * lax.rsqrt(ms + EPS) * g


def _rms_head64(y, g):
    lo = _lane_iota() < HEAD_DIM
    x2 = y * y
    sa = jnp.sum(jnp.where(lo, x2, 0.0), axis=-1, keepdims=True)
    sb = jnp.sum(jnp.where(lo, 0.0, x2), axis=-1, keepdims=True)
    ms = jnp.where(lo, sa, sb) * (1.0 / HEAD_DIM)
    return y * lax.rsqrt(ms + EPS) * g


def _rms_head64_mxu(y, g, ones_bd):
    x2 = y * y
    hi = x2.astype(BF16)
    lo = (x2 - hi.astype(F32)).astype(BF16)
    ms = (_dot(hi, ones_bd) + _dot(lo, ones_bd)) * (1.0 / HEAD_DIM)
    return y * lax.rsqrt(ms + EPS) * g


def _rope(y, cos, sin, half):
    first = (_lane_iota() % (2 * half)) < half
    sw = jnp.where(first, pltpu.roll(y, LANES - half, 1), pltpu.roll(y, half, 1))
    return y * cos + sw * sin


def _dup_halves(y):
    lo = _lane_iota() < HEAD_DIM
    yr = pltpu.roll(y, HEAD_DIM, 1)
    return jnp.where(lo, y, yr), jnp.where(lo, yr, y)


def _exp2_parts(parts, sink=None):
    m = None
    for s in parts:
        mi = jnp.max(s, axis=-1, keepdims=True)
        m = mi if m is None else jnp.maximum(m, mi)
    if sink is not None:
        m = jnp.maximum(m, sink)
    es = [jnp.exp2(s - m).astype(BF16) for s in parts]
    return es, (None if sink is None else jnp.exp2(sink - m))


def _pv(es, vals):
    o = None
    for e, v in zip(es, vals):
        oi = _dot(e, v)
        o = oi if o is None else o + oi
    return o


def _norm_lo(o, extra=None):
    den = pltpu.roll(o, HEAD_DIM, 1)
    return o / (den if extra is None else den + extra)


def _norm_hi(o, extra=None):
    return pltpu.roll(o, HEAD_DIM, 1) / (o if extra is None else o + extra)


def _pipelined(n, scores_fn, finish_fn):
    nxt = scores_fn(0)
    for i in range(n):
        cur = nxt
        if i + 1 < n:
            nxt = scores_fn(i + 1)
        finish_fn(i, cur)


def _with_ones(v):
    lo = _lane_iota() < HEAD_DIM
    vf = v.astype(F32)
    one = jnp.ones_like(vf)
    return (jnp.where(lo, vf, one).astype(BF16),
            jnp.where(lo, pltpu.roll(vf, HEAD_DIM, 1), one).astype(BF16))


def _ada_kernel(c_ref, w_ref, b_ref, o_ref):
    cond = _silu(c_ref[...]).astype(BF16)
    o_ref[0] = _dot(cond, w_ref[0].astype(BF16)) + b_ref[0]


def _ada(cvec, w_ada, b_ada):
    rows = cvec.shape[0]
    nblk = 6
    return pl.pallas_call(
        _ada_kernel,
        out_shape=jax.ShapeDtypeStruct((DEPTH, rows, 6 * D), F32),
        grid=(DEPTH, nblk),
        in_specs=[pl.BlockSpec((rows, D), lambda i, j: (0, 0)),
                  pl.BlockSpec((1, D, D), lambda i, j: (i, 0, j)),
                  pl.BlockSpec((1, 1, D), lambda i, j: (i, 0, j))],
        out_specs=pl.BlockSpec((1, rows, D), lambda i, j: (i, 0, j)),
        compiler_params=_cparams(2),
        name="ada",
    )(cvec, w_ada, b_ada.reshape(DEPTH, 1, 6 * D))


def _mod_spec(mod, batch_axis):
    if mod.shape[0] == 1:
        return pl.BlockSpec((1, 6, D), lambda *ids: (0, 0, 0))
    return pl.BlockSpec((1, 6, D), lambda *ids: (ids[batch_axis], 0, 0))


def _row_tile(L):
    return min(L, 512)


def _rope_table_specs(cos, sin, tm, rope):
    if rope:
        spec = pl.BlockSpec((tm, LANES), lambda b, i: (i, 0))
        return [spec, spec], [cos, sin]
    spec = pl.BlockSpec((tm, LANES), lambda b, i: (0, 0))
    return [spec, spec], [cos[:tm], sin[:tm]]


FFN_CHUNK = 256


def _tail_kernel(*refs, with_oproj):
    if with_oproj:
        o_ref, wo_ref, h_ref, mod_ref, g_ref, wgu_ref, wd_ref, out_ref, act_ref = refs
        h = h_ref[0] + mod_ref[0, 2:3, :] * _dot(o_ref[0], wo_ref[...])
    else:
        h_ref, mod_ref, g_ref, wgu_ref, wd_ref, out_ref, act_ref = refs
        h = h_ref[0]
    a = _modnorm(h, g_ref[...], mod_ref[0, 3:4, :], mod_ref[0, 4:5, :]).astype(BF16)
    for c in range(D_FF // FFN_CHUNK):
        lo = c * FFN_CHUNK
        gate = _dot(a, wgu_ref[:, lo:lo + FFN_CHUNK])
        up = _dot(a, wgu_ref[:, D_FF + lo:D_FF + lo + FFN_CHUNK])
        act_ref[:, lo:lo + FFN_CHUNK] = (_silu(gate) * up).astype(BF16)
    y = _dot(act_ref[...], wd_ref[...])
    out_ref[0] = h + mod_ref[0, 5:6, :] * y


def _tail(o, wo, h, mod, g, wgu, wd):
    B, L, _ = h.shape
    tm = _row_tile(L)
    row = pl.BlockSpec((1, tm, D), lambda b, i: (b, i, 0))
    with_oproj = o is not None
    specs = [row, _mod_spec(mod, 0), _const_spec((1, D)), _const_spec((D, 2 * D_FF)), _const_spec((D_FF, D))]
    args = [h, mod, g.reshape(1, D), wgu, wd]
    if with_oproj:
        specs = [row, _const_spec((D, D))] + specs
        args = [o, wo] + args
    return pl.pallas_call(
        functools.partial(_tail_kernel, with_oproj=with_oproj),
        out_shape=jax.ShapeDtypeStruct(h.shape, F32),
        grid=(B, L // tm),
        in_specs=specs,
        out_specs=row,
        scratch_shapes=[pltpu.VMEM((tm, D_FF), BF16)],
        compiler_params=_cparams(2),
        name="tail" if with_oproj else "ffn",
    )(*args)


def _axial_tables(L, half, lead, width):
    t = np.arange(L)
    freqs = ROPE_THETA ** (-np.arange(half, dtype=np.float64) / half)
    cos = np.ones((L, LANES), np.float64)
    sin = np.zeros((L, LANES), np.float64)
    for base in range(0, LANES, width):
        for axis, pos in enumerate((t // GRID_W, t % GRID_W)):
            ang = pos[:, None].astype(np.float64) * freqs[None, :]
            o = base + lead + axis * 2 * half
            cos[:, o:o + half] = np.cos(ang)
            cos[:, o + half:o + 2 * half] = np.cos(ang)
            sin[:, o:o + half] = -np.sin(ang)
            sin[:, o + half:o + 2 * half] = np.sin(ang)
    return jnp.asarray(cos, F32), jnp.asarray(sin, F32)


SWA_NQ = SWA_HEADS * HEAD_DIM
SWA_NK = SWA_KV_HEADS * HEAD_DIM
SWA_SCALE = HEAD_DIM ** -0.5 * LOG2E


def _swa_proj_kernel(h_ref, mod_ref, g_ref, w_ref, gq_ref, gk_ref, ones_ref, cos_ref, sin_ref,
                     q_ref, k_ref, v_ref, *, rope):
    a = _modnorm(h_ref[0], g_ref[...], mod_ref[0, 0:1, :], mod_ref[0, 1:2, :]).astype(BF16)
    ones_bd = ones_ref[...]
    lo = _lane_iota() < HEAD_DIM
    n_q = SWA_NQ // PROJ_CHUNK

    def normed(blk, gain):
        blk = _rms_head64_mxu(blk, gain, ones_bd)
        return _rope(blk, cos_ref[...], sin_ref[...], HEAD_DIM // 4) if rope else blk

    def matmul(c):
        return _dot(a, w_ref[:, c * PROJ_CHUNK:(c + 1) * PROJ_CHUNK])

    def epilogue(c, y):
        for half in range(PROJ_CHUNK // LANES):
            blk = y[:, half * LANES:(half + 1) * LANES]
            if c < n_q:
                cb = c * (PROJ_CHUNK // LANES) + half
                q_ref[0, :, cb * LANES:(cb + 1) * LANES] = normed(blk, gq_ref[...]).astype(BF16)
                continue
            out_ref = k_ref if c == n_q else v_ref
            if c == n_q:
                first, second = _dup_halves(normed(blk, gk_ref[...]))
            else:
                first, second = jnp.where(lo, blk, 1.0), jnp.where(lo, pltpu.roll(blk, HEAD_DIM, 1), 1.0)
            out_ref[0, :, (2 * half) * LANES:(2 * half + 1) * LANES] = first.astype(BF16)
            out_ref[0, :, (2 * half + 1) * LANES:(2 * half + 2) * LANES] = second.astype(BF16)

    _pipelined(n_q + 2, matmul, epilogue)


def _head_ones():
    half = np.arange(LANES) // HEAD_DIM
    return jnp.asarray(half[:, None] == half[None, :], BF16)


def _swa_proj(h, mod, g, w, gq, gk, cos, sin, rope):
    B, L, _ = h.shape
    tm = _row_tile(L)
    nkv = 2 * SWA_NK
    row = lambda b, i: (b, i, 0)
    table_specs, tables = _rope_table_specs(cos, sin, tm, rope)
    return pl.pallas_call(
        functools.partial(_swa_proj_kernel, rope=rope),
        out_shape=(jax.ShapeDtypeStruct((B, L, SWA_NQ), BF16),
                   jax.ShapeDtypeStruct((B, L, nkv), BF16),
                   jax.ShapeDtypeStruct((B, L, nkv), BF16)),
        grid=(B, L // tm),
        in_specs=[pl.BlockSpec((1, tm, D), row),
                  _mod_spec(mod, 0),
                  _const_spec((1, D)),
                  _const_spec((D, SWA_NQ + 2 * SWA_NK)),
                  _const_spec((1, LANES)),
                  _const_spec((1, LANES)),
                  _const_spec((LANES, LANES))] + table_specs,
        out_specs=(pl.BlockSpec((1, tm, SWA_NQ), row),
                   pl.BlockSpec((1, tm, nkv), row),
                   pl.BlockSpec((1, tm, nkv), row)),
        compiler_params=_cparams(2),
        name="swa_proj",
    )(h, mod, g.reshape(1, D), w, (jnp.tile(gq, 2) * SWA_SCALE).reshape(1, LANES),
      jnp.tile(gk, 2).reshape(1, LANES), _head_ones(), *tables)


SWA_QB = 256
SWA_SPAN = SWA_QB + 2 * SWA_WINDOW
SWA_STEP_BLOCKS = 4


def _swa_band_masks():
    iq = np.arange(SWA_QB)[:, None]
    ik = np.arange(SWA_SPAN)[None, :]
    masks = [np.where(np.abs(o * SWA_WINDOW + iq - ik) <= SWA_WINDOW, 0.0, NEG_INF) for o in range(3)]
    return jnp.asarray(np.stack(masks), F32)


def _swa_attn_kernel(*refs, with_window, L, nblocks):
    if with_window:
        q_ref, kd_ref, vd_ref, kc_ref, vc_ref, sink_ref, band_ref, o_ref = refs
    else:
        q_ref, kc_ref, vc_ref, sink_ref, o_ref = refs
    lo = _lane_iota() < HEAD_DIM

    def window_start(bi):
        start = (pl.program_id(1) * nblocks + bi) * SWA_QB
        w0 = pl.multiple_of(jnp.clip(start - SWA_WINDOW, 0, L - SWA_SPAN), SWA_WINDOW)
        return start, w0

    def keys_or_values(win_ref, ctx_ref, bi, cols):
        if not with_window:
            return ctx_ref[0, :, cols]
        _, w0 = window_start(bi)
        return jnp.concatenate([win_ref[0, pl.ds(w0, SWA_SPAN), cols], ctx_ref[0, :, cols]], axis=0)

    def scores(c):
        bi, hk = divmod(c, SWA_KV_HEADS)
        qrows = slice(bi * SWA_QB, (bi + 1) * SWA_QB)
        qs = []
        for g in range(SWA_GROUPS):
            cb = hk * 2 + g // 2
            qc = q_ref[0, qrows, cb * LANES:(cb + 1) * LANES]
            zero = jnp.zeros_like(qc)
            qs.append(jnp.where(lo, qc, zero) if g % 2 == 0 else jnp.where(lo, zero, qc))
        qstack = jnp.concatenate(qs, axis=0)
        cols = slice(hk * LANES, (hk + 1) * LANES)
        s = _dot_nt(qstack, keys_or_values(kd_ref if with_window else None, kc_ref, bi, cols))
        if with_window:
            start, w0 = window_start(bi)
            band = band_ref[(start - w0) // SWA_WINDOW]
            band = jnp.concatenate([band] * SWA_GROUPS, axis=0)
            s = jnp.concatenate([s[:, :SWA_SPAN] + band, s[:, SWA_SPAN:]], axis=1)
        return [s]

    def finish(c, parts):
        bi, hk = divmod(c, SWA_KV_HEADS)
        qrows = slice(bi * SWA_QB, (bi + 1) * SWA_QB)
        cols = slice(hk * LANES, (hk + 1) * LANES)
        vals = [keys_or_values(vd_ref if with_window else None, vc_ref, bi, cols)]
        es, sink_e = [], []
        for g in range(SWA_GROUPS):
            head_rows = slice(g * SWA_QB, (g + 1) * SWA_QB)
            eg, sg = _exp2_parts([s[head_rows] for s in parts], sink_ref[hk * SWA_GROUPS + g] * LOG2E)
            es.append(eg[0])
            sink_e.append(sg)
        o = _pv([jnp.concatenate(es, axis=0)], vals)
        for j in range(2):
            even = slice((2 * j) * SWA_QB, (2 * j + 1) * SWA_QB)
            odd = slice((2 * j + 1) * SWA_QB, (2 * j + 2) * SWA_QB)
            pair = jnp.where(lo, _norm_lo(o[even], sink_e[2 * j]), _norm_hi(o[odd], sink_e[2 * j + 1]))
            o_ref[0, qrows, (hk * 2 + j) * LANES:(hk * 2 + j + 1) * LANES] = pair.astype(BF16)

    _pipelined(nblocks * SWA_KV_HEADS, scores, finish)


def _swa_attn(q, kd, vd, kcd, vcd, sink):
    B, L, _ = q.shape
    Lc = kcd.shape[1]
    nkv = 2 * SWA_NK
    with_window = kd is not None
    nblocks = min(SWA_STEP_BLOCKS, L // SWA_QB)
    blk = lambda b, i: (b, i, 0)
    whole = lambda b, i: (b, 0, 0)
    specs = [pl.BlockSpec((1, nblocks * SWA_QB, SWA_NQ), blk)]
    args = [q]
    if with_window:
        specs += [pl.BlockSpec((1, L, nkv), whole), pl.BlockSpec((1, L, nkv), whole)]
        args += [kd, vd]
    specs += [pl.BlockSpec((1, Lc, nkv), whole), pl.BlockSpec((1, Lc, nkv), whole),
              pl.BlockSpec(memory_space=pltpu.SMEM)]
    args += [kcd, vcd, sink]
    if with_window:
        specs.append(_const_spec((3, SWA_QB, SWA_SPAN)))
        args.append(_swa_band_masks())
    return pl.pallas_call(
        functools.partial(_swa_attn_kernel, with_window=with_window, L=L, nblocks=nblocks),
        out_shape=jax.ShapeDtypeStruct((B, L, SWA_NQ), BF16),
        grid=(B, L // (nblocks * SWA_QB)),
        in_specs=specs,
        out_specs=pl.BlockSpec((1, nblocks * SWA_QB, SWA_NQ), blk),
        compiler_params=_cparams(2),
        name="swa_attn" if with_window else "swa_attn_ctx",
    )(*args)


NA_N = NA_HEADS * HEAD_DIM
NA_SCALE = HEAD_DIM ** -0.5 * LOG2E


def _na_proj_kernel(h_ref, mod_ref, g_ref, w_ref, gq_ref, gk_ref, q_ref, k_ref, v_ref):
    a = _modnorm(h_ref[0], g_ref[...], mod_ref[0, 0:1, :], mod_ref[0, 1:2, :]).astype(BF16)
    per_kind = NA_N // PROJ_CHUNK

    def matmul(c):
        return _dot(a, w_ref[:, c * PROJ_CHUNK:(c + 1) * PROJ_CHUNK])

    def epilogue(c, y):
        kind, cc = divmod(c, per_kind)
        for half in range(PROJ_CHUNK // LANES):
            blk = y[:, half * LANES:(half + 1) * LANES]
            cols = slice(cc * PROJ_CHUNK + half * LANES, cc * PROJ_CHUNK + (half + 1) * LANES)
            if kind == 0:
                q_ref[0, :, cols] = _rms_head64(blk, gq_ref[...]).astype(BF16)
            elif kind == 1:
                k_ref[0, :, cols] = _rms_head64(blk, gk_ref[...]).astype(BF16)
            else:
                v_ref[0, :, cols] = blk.astype(BF16)

    _pipelined(3 * per_kind, matmul, epilogue)


def _na_proj(h, mod, g, w, gq, gk):
    B, L, _ = h.shape
    tm = _row_tile(L)
    row = lambda b, i: (b, i, 0)
    out = jax.ShapeDtypeStruct((B, L, NA_N), BF16)
    return pl.pallas_call(
        _na_proj_kernel,
        out_shape=(out, out, out),
        grid=(B, L // tm),
        in_specs=[pl.BlockSpec((1, tm, D), row),
                  _mod_spec(mod, 0),
                  _const_spec((1, D)),
                  _const_spec((D, 3 * NA_N)),
                  _const_spec((1, LANES)),
                  _const_spec((1, LANES))],
        out_specs=(pl.BlockSpec((1, tm, NA_N), row),) * 3,
        compiler_params=_cparams(2),
        name="na_proj",
    )(h, mod, g.reshape(1, D), w, (jnp.tile(gq, 2) * NA_SCALE).reshape(1, LANES),
      jnp.tile(gk, 2).reshape(1, LANES))


NA_PAIRS = NA_WROWS // 2
NA_OFFSETS = 2 * NA_KH


def _na_bias_tiles(rpb, kinds):
    H = rpb.shape[0]
    c = np.arange(GRID_W)
    col_start = np.clip(c - NA_KW // 2, 0, GRID_W - NA_KW)
    col_ok = (c[None, :] >= col_start[:, None]) & (c[None, :] < col_start[:, None] + NA_KW)
    dc = np.clip(c[None, :] - c[:, None], -(NA_KW - 1), NA_KW - 1) + NA_KW - 1
    onehot = (dc[None] == np.arange(2 * NA_KW - 1)[:, None, None]).astype(np.float32)
    e = jnp.einsum("hrd,dqk->hrqk", rpb.astype(F32) * LOG2E, jnp.asarray(onehot),
                   precision=lax.Precision.HIGHEST)
    e = jnp.where(jnp.asarray(col_ok), e, NEG_INF)
    masked = jnp.full((H, 1, GRID_W, GRID_W), NEG_INF, F32)
    ext = jnp.concatenate([masked, e, masked], axis=1)
    off = masked[:, 0]
    tiles = []
    for variant, shifted in kinds:
        first = ext[:, shifted] if variant != 2 else off
        second = ext[:, shifted + 1] if variant != 1 else off
        tiles.append(jnp.concatenate([first, second], axis=-1))
    return jnp.stack(tiles, axis=1)


def _na_tile_ids(rows):
    kh = min(NA_KH, rows)
    nblk = rows // NA_QROWS
    ids = np.zeros((nblk, NA_QROWS, NA_PAIRS), np.int32)
    kinds = []
    starts = []
    for g in range(nblk):
        ws = int(np.clip(NA_QROWS * g - NA_QROWS, 0, rows - NA_WROWS))
        starts.append(ws)
        for i in range(NA_QROWS):
            r = NA_QROWS * g + i
            r0 = int(np.clip(r - kh // 2, 0, rows - kh))
            for kp in range(NA_PAIRS):
                rk = ws + 2 * kp
                ok1 = r0 <= rk < r0 + kh
                ok2 = r0 <= rk + 1 < r0 + kh
                shifted = rk - r + NA_KH - 1 + 1
                if ok1 and ok2:
                    variant = 0
                elif ok1:
                    variant = 1
                elif ok2:
                    variant = 2
                else:
                    variant, shifted = 1, 0
                assert 0 <= shifted < NA_OFFSETS
                if (variant, shifted) not in kinds:
                    kinds.append((variant, shifted))
                ids[g, i, kp] = kinds.index((variant, shifted))
    return ids, kinds, starts


def _na_attn_kernel(*refs, with_lat, npairs, L):
    lo = _lane_iota() < HEAD_DIM
    if with_lat:
        q_ref, k_ref, v_ref, kc_ref, vc_ref, tile_ref, o_ref, v1_ref = refs
        ids, _, starts = _na_tile_ids(L // GRID_W)
        tq = NA_QB
    else:
        q_ref, kc_ref, vc_ref, o_ref = refs
        tq = L
    nblk = L // tq
    if with_lat:
        v1_ref[0], v1_ref[1] = _with_ones(v_ref[0])
    vc_ones = [_with_ones(vc_ref[0, :, p * LANES:(p + 1) * LANES]) for p in range(npairs)]

    def window(g):
        return slice(starts[g] * GRID_W, starts[g] * GRID_W + NA_WB)

    def scores(c):
        p, g = divmod(c, nblk)
        cols = slice(p * LANES, (p + 1) * LANES)
        q = q_ref[0, g * tq:(g + 1) * tq, cols]
        zero = jnp.zeros_like(q)
        qstack = jnp.concatenate([jnp.where(lo, q, zero), jnp.where(lo, zero, q)], axis=0)
        if not with_lat:
            return [_dot_nt(qstack, kc_ref[0, :, cols])]
        s = _dot_nt(qstack, jnp.concatenate([k_ref[0, window(g), cols], kc_ref[0, :, cols]], axis=0))
        bias = jnp.concatenate(
            [jnp.concatenate([tile_ref[hh, int(ids[g, i, kp])] for kp in range(NA_PAIRS)], axis=1)
             for hh in range(2) for i in range(NA_QROWS)], axis=0)
        return [jnp.concatenate([s[:, :NA_WB] + bias, s[:, NA_WB:]], axis=1)]

    def finish(c, parts):
        p, g = divmod(c, nblk)
        es, _ = _exp2_parts(parts)
        outs = []
        for hh in range(2):
            vals = vc_ones[p][hh]
            if with_lat:
                vals = jnp.concatenate([v1_ref[hh, window(g), :], vals], axis=0)
            outs.append(_pv([e[hh * tq:(hh + 1) * tq] for e in es], [vals]))
        pair = jnp.where(lo, _norm_lo(outs[0]), _norm_hi(outs[1]))
        o_ref[0, g * tq:(g + 1) * tq, p * LANES:(p + 1) * LANES] = pair.astype(BF16)

    _pipelined(npairs * nblk, scores, finish)


def _na_attn(q, k, v, kc, vc, tiles):
    B, L, _ = q.shape
    Lc = kc.shape[1]
    with_lat = k is not None
    scratch = []
    if with_lat:
        npairs = 1
        grid = (NA_HEADS // 2, B)
        blk = lambda n: pl.BlockSpec((1, n, LANES), lambda p, b: (b, 0, p))
        specs = [blk(L), blk(L), blk(L), blk(Lc), blk(Lc),
                 pl.BlockSpec((2,) + tiles.shape[1:], lambda p, b: (p, 0, 0, 0))]
        args = [q, k, v, kc, vc, tiles]
        scratch = [pltpu.VMEM((2, L, LANES), BF16)]
        out_spec = blk(L)
    else:
        npairs = NA_HEADS // 2
        grid = (B,)
        blk = lambda n: pl.BlockSpec((1, n, NA_N), lambda b: (b, 0, 0))
        specs = [blk(L), blk(Lc), blk(Lc)]
        args = [q, kc, vc]
        out_spec = blk(L)
    return pl.pallas_call(
        functools.partial(_na_attn_kernel, with_lat=with_lat, npairs=npairs, L=L),
        out_shape=jax.ShapeDtypeStruct((B, L, NA_N), BF16),
        grid=grid,
        in_specs=specs,
        out_specs=out_spec,
        scratch_shapes=scratch,
        compiler_params=_cparams(len(grid)),
        name="na_attn" if with_lat else "na_attn_ctx",
    )(*args)


POOL_HALO = SUBLANES


def _pool_bands(tm):
    t = np.arange(tm)[:, None]
    j = np.arange(tm + 2 * POOL_HALO)[None, :] - POOL_HALO
    return jnp.asarray(np.stack([(j >= t - w // 2) & (j < t - w // 2 + w) for w in POOL_WINDOWS]), BF16)


def _pool_kernel(h_ref, hp_ref, hn_ref, mod_ref, g_ref, band_ref, w_ref, b_ref, sc_ref, o_ref, *, tm, L):
    i = pl.program_id(1)
    nt = L // tm
    g = g_ref[...]
    shift, scale = mod_ref[0, 0:1, :], mod_ref[0, 1:2, :]
    h = h_ref[0]
    a = _modnorm(h, g, shift, scale)
    prev_on = (i > 0).astype(F32)
    next_on = (i < nt - 1).astype(F32)
    ext = jnp.concatenate([_modnorm(hp_ref[0], g, shift, scale) * prev_on, a,
                           _modnorm(hn_ref[0], g, shift, scale) * next_on], axis=0)
    ext_hi = ext.astype(BF16)
    ext_lo = (ext - ext_hi.astype(F32)).astype(BF16)
    t = i * tm + lax.broadcasted_iota(jnp.int32, (tm, 1), 0)
    group_cols = [slice(gi * POOL_DG, (gi + 1) * POOL_DG) for gi in range(len(POOL_WINDOWS))]
    sums = [_dot(band_ref[gi], ext_hi[:, cols]) + _dot(band_ref[gi], ext_lo[:, cols])
            for gi, cols in enumerate(group_cols)]
    pooled = []
    for gi, w in enumerate(POOL_WINDOWS):
        first = jnp.clip(t - w // 2, 0, L)
        last = jnp.clip(t - w // 2 + w, 0, L)
        cnt = (last - first).astype(F32)
        pooled.append((sums[gi] / cnt - a[:, group_cols[gi]]).astype(BF16))
    ys = [_dot(pooled[gi], w_ref[gi]) + b_ref[gi:gi + 1, :] for gi in range(len(POOL_WINDOWS))]
    y = jnp.concatenate(ys, axis=-1) * sc_ref[...]
    o_ref[0] = h + mod_ref[0, 2:3, :] * y


def _pool(h, mod, g, w, b, sc):
    B, L, _ = h.shape
    tm = min(L, 256)
    hb = tm // POOL_HALO
    last_halo = L // POOL_HALO - 1
    return pl.pallas_call(
        functools.partial(_pool_kernel, tm=tm, L=L),
        out_shape=jax.ShapeDtypeStruct(h.shape, F32),
        grid=(B, L // tm),
        in_specs=[pl.BlockSpec((1, tm, D), lambda b_, i: (b_, i, 0)),
                  pl.BlockSpec((1, POOL_HALO, D), lambda b_, i: (b_, jnp.maximum(i * hb - 1, 0), 0)),
                  pl.BlockSpec((1, POOL_HALO, D), lambda b_, i: (b_, jnp.minimum((i + 1) * hb, last_halo), 0)),
                  _mod_spec(mod, 0),
                  _const_spec((1, D)),
                  _const_spec((len(POOL_WINDOWS), tm, tm + 2 * POOL_HALO)),
                  _const_spec((len(POOL_WINDOWS), POOL_DG, POOL_DG)),
                  _const_spec((len(POOL_WINDOWS), POOL_DG)),
                  _const_spec((1, D))],
        out_specs=pl.BlockSpec((1, tm, D), lambda b_, i: (b_, i, 0)),
        compiler_params=_cparams(2),
        name="pool",
    )(h, h, h, mod, g.reshape(1, D), _pool_bands(tm), w, b, sc.reshape(1, D))


MLA_A_COLS = 512
MLA_HQ = MLA_HEADS * LANES
MLA_SCALE = MLA_QK ** -0.5 * LOG2E


def _mla_proj_kernel(*refs, rope, need_q):
    (h_ref, mod_ref, g_ref, wa_ref, gcq_ref, gckv_ref, wuq_ref, wuk_ref, wuv_ref,
     gq_ref, gk_ref, ones_ref, cos_ref, sin_ref) = refs[:14]
    outs = refs[14:]
    if need_q:
        q_ref, k_ref, v_ref = outs
    else:
        k_ref, v_ref = outs
    a = _modnorm(h_ref[0], g_ref[...], mod_ref[0, 0:1, :], mod_ref[0, 1:2, :]).astype(BF16)
    proj = _dot(a, wa_ref[...])
    def rotary(blk):
        return blk * cos_ref[...] + pltpu.roll(blk, LANES // 2, 1) * sin_ref[...] if rope else blk

    if need_q:
        cq = _rms_full(proj[:, :MLA_Q_RANK], gcq_ref[...], MLA_Q_RANK).astype(BF16)
        ones = ones_ref[...]
    ckv = _rms_full(proj[:, MLA_Q_RANK:MLA_Q_RANK + MLA_KV_RANK], gckv_ref[...], MLA_KV_RANK).astype(BF16)
    kr = proj[:, MLA_Q_RANK + MLA_KV_RANK:]
    kr_rot = rotary(kr * gk_ref[...])
    kr_ss = jnp.sum(kr * kr, axis=-1, keepdims=True)

    def q_head(blk):
        x2 = blk * blk
        hi = x2.astype(BF16)
        lo = (x2 - hi.astype(F32)).astype(BF16)
        ms = (_dot(hi, ones) + _dot(lo, ones)) * (1.0 / MLA_QK)
        return rotary(blk * lax.rsqrt(ms + EPS) * gq_ref[...])

    def k_head(blk):
        ms = (jnp.sum(blk * blk, axis=-1, keepdims=True) + kr_ss) * (1.0 / MLA_QK)
        return (blk * gk_ref[...] + kr_rot) * lax.rsqrt(ms + EPS)

    heads_per_chunk = PROJ_CHUNK // LANES
    chunks = []
    for j in range(MLA_HEADS // heads_per_chunk):
        chunks += ([("q", j)] if need_q else []) + [("k", j)]
    chunks += [("v", j) for j in range(MLA_HEADS * MLA_V // PROJ_CHUNK)]

    def matmul(c):
        kind, j = chunks[c]
        cols = slice(j * PROJ_CHUNK, (j + 1) * PROJ_CHUNK)
        if kind == "q":
            return _dot(cq, wuq_ref[:, cols])
        return _dot(ckv, wuk_ref[:, cols] if kind == "k" else wuv_ref[:, cols])

    def epilogue(c, y):
        kind, j = chunks[c]
        if kind == "v":
            v_ref[0, :, j * PROJ_CHUNK:(j + 1) * PROJ_CHUNK] = y.astype(BF16)
            return
        for half in range(heads_per_chunk):
            blk = y[:, half * LANES:(half + 1) * LANES]
            cols = slice((j * heads_per_chunk + half) * LANES, (j * heads_per_chunk + half + 1) * LANES)
            if kind == "q":
                q_ref[0, :, cols] = q_head(blk).astype(BF16)
            else:
                k_ref[0, :, cols] = k_head(blk).astype(BF16)

    _pipelined(len(chunks), matmul, epilogue)


def _mla_proj(h, mod, g, wts, cos, sin, rope, need_q):
    B, L, _ = h.shape
    tm = _row_tile(L)
    row = lambda b, i: (b, i, 0)
    out_shape = [jax.ShapeDtypeStruct((B, L, MLA_HQ), BF16), jax.ShapeDtypeStruct((B, L, MLA_HEADS * MLA_V), BF16)]
    out_specs = [pl.BlockSpec((1, tm, MLA_HQ), row), pl.BlockSpec((1, tm, MLA_HEADS * MLA_V), row)]
    if need_q:
        out_shape.insert(0, jax.ShapeDtypeStruct((B, L, MLA_HQ), BF16))
        out_specs.insert(0, pl.BlockSpec((1, tm, MLA_HQ), row))
    table_specs, tables = _rope_table_specs(cos, sin, tm, rope)
    return pl.pallas_call(
        functools.partial(_mla_proj_kernel, rope=rope, need_q=need_q),
        out_shape=tuple(out_shape),
        grid=(B, L // tm),
        in_specs=[pl.BlockSpec((1, tm, D), row),
                  _mod_spec(mod, 0),
                  _const_spec((1, D)),
                  _const_spec((D, MLA_A_COLS)),
                  _const_spec((1, MLA_Q_RANK)),
                  _const_spec((1, MLA_KV_RANK)),
                  _const_spec((MLA_Q_RANK, MLA_HQ)),
                  _const_spec((MLA_KV_RANK, MLA_HQ)),
                  _const_spec((MLA_KV_RANK, MLA_HEADS * MLA_V)),
                  _const_spec((1, LANES)),
                  _const_spec((1, LANES)),
                  _const_spec((LANES, LANES))] + table_specs,
        out_specs=tuple(out_specs),
        compiler_params=_cparams(2),
        name="mla_proj" if need_q else "mla_proj_kv",
    )(h, mod, g.reshape(1, D), wts["wa"], wts["gcq"], wts["gckv"], wts["wuq"], wts["wuk"], wts["wuv"],
      wts["gq"], wts["gk"], jnp.ones((LANES, LANES), BF16), *tables)


def _mla_head_lanes(nope, rope):
    q = MLA_ROPE // 4
    ref = nope if nope is not None else rope
    zeros = lambda n: jnp.zeros(ref.shape[:-1] + (n,), ref.dtype)
    nope_a, nope_b = (zeros(48), zeros(16)) if nope is None else (nope[..., :48], nope[..., 48:])
    if rope is None:
        x1, x2 = zeros(2 * q), zeros(2 * q)
    else:
        x1 = jnp.concatenate([rope[..., 0:q], rope[..., 2 * q:3 * q]], axis=-1)
        x2 = jnp.concatenate([rope[..., q:2 * q], rope[..., 3 * q:4 * q]], axis=-1)
    return jnp.concatenate([nope_a, x1, nope_b, zeros(32), x2], axis=-1)


def _mla_rope_tables(L):
    quarter = MLA_ROPE // 4
    t = np.arange(L)
    freqs = ROPE_THETA ** (-np.arange(quarter, dtype=np.float64) / quarter)
    cos = np.ones((L, LANES), np.float64)
    sin = np.zeros((L, LANES), np.float64)
    for axis, pos in enumerate((t // GRID_W, t % GRID_W)):
        ang = pos[:, None].astype(np.float64) * freqs[None, :]
        x1 = slice(48 + axis * quarter, 48 + (axis + 1) * quarter)
        x2 = slice(112 + axis * quarter, 112 + (axis + 1) * quarter)
        cos[:, x1] = np.cos(ang)
        cos[:, x2] = np.cos(ang)
        sin[:, x1] = -np.sin(ang)
        sin[:, x2] = np.sin(ang)
    return jnp.asarray(cos, F32), jnp.asarray(sin, F32)


def _mla_weights(w_a, g_cq, g_ckv, w_uq, w_ukv, g_q, g_k):
    H = MLA_HEADS
    spread = lambda t: _mla_head_lanes(t[..., :MLA_NOPE], t[..., MLA_NOPE:])
    n_lin = MLA_Q_RANK + MLA_KV_RANK
    wa = jnp.concatenate([w_a[:, :n_lin], _mla_head_lanes(None, w_a[:, n_lin:])], axis=1)
    wukv = w_ukv.reshape(MLA_KV_RANK, H, MLA_NOPE + MLA_V)
    wuk = _mla_head_lanes(wukv[:, :, :MLA_NOPE], None)
    return dict(wa=wa.astype(BF16),
                gcq=g_cq.reshape(1, MLA_Q_RANK), gckv=g_ckv.reshape(1, MLA_KV_RANK),
                wuq=spread(w_uq.reshape(MLA_Q_RANK, H, MLA_QK)).reshape(MLA_Q_RANK, MLA_HQ).astype(BF16),
                wuk=wuk.reshape(MLA_KV_RANK, MLA_HQ).astype(BF16),
                wuv=wukv[:, :, MLA_NOPE:].reshape(MLA_KV_RANK, H * MLA_V).astype(BF16),
                gq=(spread(g_q) * MLA_SCALE).reshape(1, LANES), gk=spread(g_k).reshape(1, LANES))


MLA_TQ = 512


def _mla_attn_kernel(q_ref, k_ref, kc_ref, v_ref, vc_ref, o_ref, kall_ref, v1_ref):
    lo = _lane_iota() < MLA_V
    L, Lc = k_ref.shape[1], kc_ref.shape[1]
    kall_ref[0:L, :] = k_ref[0]
    kall_ref[L:L + Lc, :] = kc_ref[0]
    v1_ref[0, 0:L, :], v1_ref[1, 0:L, :] = _with_ones(v_ref[0])
    v1_ref[0, L:L + Lc, :], v1_ref[1, L:L + Lc, :] = _with_ones(vc_ref[0])
    first_head = []

    def scores(c):
        qi, j = divmod(c, 2)
        cols = slice(j * LANES, (j + 1) * LANES)
        return [_dot_nt(q_ref[0, qi * MLA_TQ:(qi + 1) * MLA_TQ, cols], kall_ref[:, cols])]

    def finish(c, parts):
        qi, j = divmod(c, 2)
        es, _ = _exp2_parts(parts)
        o = _pv(es, [v1_ref[j]])
        if j == 0:
            first_head.append(_norm_lo(o))
        else:
            pair = jnp.where(lo, first_head.pop(), _norm_hi(o))
            o_ref[0, qi * MLA_TQ:(qi + 1) * MLA_TQ, :] = pair.astype(BF16)

    _pipelined(2 * (q_ref.shape[1] // MLA_TQ), scores, finish)


def _mla_attn(q, k, kc, v, vc):
    B, L, _ = q.shape
    Lc = kc.shape[1]
    npair = MLA_HEADS // 2
    return pl.pallas_call(
        _mla_attn_kernel,
        out_shape=jax.ShapeDtypeStruct((B, L, MLA_HEADS * MLA_V), BF16),
        grid=(B, npair),
        in_specs=[pl.BlockSpec((1, L, 2 * LANES), lambda b, p: (b, 0, p)),
                  pl.BlockSpec((1, L, 2 * LANES), lambda b, p: (b, 0, p)),
                  pl.BlockSpec((1, Lc, 2 * LANES), lambda b, p: (b, 0, p)),
                  pl.BlockSpec((1, L, LANES), lambda b, p: (b, 0, p)),
                  pl.BlockSpec((1, Lc, LANES), lambda b, p: (b, 0, p))],
        out_specs=pl.BlockSpec((1, L, LANES), lambda b, p: (b, 0, p)),
        scratch_shapes=[pltpu.VMEM((L + Lc, 2 * LANES), BF16), pltpu.VMEM((2, L + Lc, LANES), BF16)],
        compiler_params=_cparams(2),
        name="mla_attn",
    )(q, k, kc, v, vc)


def kernel(x, c, ctx, c_ctx, w_ada, b_ada, g_mix, g_ffn, w_gate_up, w_down,
           swa_w_qkv, swa_g_q, swa_g_k, swa_sink, swa_w_o,
           na_w_qkv, na_g_q, na_g_k, na_rpb, na_w_o,
           pool_w, pool_b, pool_scale,
           mla_w_a, mla_g_cq, mla_g_ckv, mla_w_uq, mla_w_ukv, mla_g_q, mla_g_k, mla_w_o):
    B, L, _ = x.shape
    n_rows = -(-(B + 1) // SUBLANES) * SUBLANES
    cvec = jnp.concatenate([c, c_ctx[None, :], jnp.zeros((n_rows - B - 1, D), F32)], axis=0)
    mods = _ada(cvec, w_ada, b_ada).reshape(DEPTH, n_rows, 6, D)

    cos_a, sin_a = _axial_tables(L, HEAD_DIM // 4, 0, HEAD_DIM)
    cos_d, sin_d = _mla_rope_tables(L)

    flat = lambda t: t.reshape(1, -1, t.shape[-1])
    per_batch = lambda t: t.reshape(B, -1, t.shape[-1])

    h, hc = x, ctx
    for i in range(DEPTH):
        m, j = i % 4, i // 4
        need_ctx = i < DEPTH - 1
        mod = mods[i, :B]
        modc = mods[i, B:B + 1]
        o = oc = wo = None
        if m == 0:
            w = swa_w_qkv[j].astype(BF16)
            wo = swa_w_o[j].astype(BF16)
            q, kd, vd = _swa_proj(h, mod, g_mix[i], w, swa_g_q[j], swa_g_k[j], cos_a, sin_a, True)
            qc, kcd, vcd = map(per_batch,
                               _swa_proj(flat(hc), modc, g_mix[i], w, swa_g_q[j], swa_g_k[j], cos_a, sin_a, False))
            o = _swa_attn(q, kd, vd, kcd, vcd, swa_sink[j])
            if need_ctx:
                oc = _swa_attn(qc, None, None, kcd, vcd, swa_sink[j])
        elif m == 1:
            w = na_w_qkv[j].astype(BF16)
            wo = na_w_o[j].astype(BF16)
            q, k, v = _na_proj(h, mod, g_mix[i], w, na_g_q[j], na_g_k[j])
            qc, kc, vc = map(per_batch, _na_proj(flat(hc), modc, g_mix[i], w, na_g_q[j], na_g_k[j]))
            o = _na_attn(q, k, v, kc, vc, _na_bias_tiles(na_rpb[j], _na_tile_ids(L // GRID_W)[1]))
            if need_ctx:
                oc = _na_attn(qc, None, None, kc, vc, None)
        elif m == 2:
            pw = pool_w[j].astype(BF16)
            h = _pool(h, mod, g_mix[i], pw, pool_b[j], pool_scale[j])
            if need_ctx:
                hc = _pool(hc, modc, g_mix[i], pw, pool_b[j], pool_scale[j])
        else:
            wts = _mla_weights(mla_w_a[j], mla_g_cq[j], mla_g_ckv[j], mla_w_uq[j], mla_w_ukv[j],
                               mla_g_q[j], mla_g_k[j])
            wo = mla_w_o[j].astype(BF16)
            assert not need_ctx, "the latent-attention mixer is only built for the last layer"
            q, k, v = _mla_proj(h, mod, g_mix[i], wts, cos_d, sin_d, True, True)
            kc, vc = map(per_batch, _mla_proj(flat(hc), modc, g_mix[i], wts, cos_d, sin_d, False, False))
            o = _mla_attn(q, k, kc, v, vc)
        wgu = w_gate_up[i].astype(BF16)
        wd = w_down[i].astype(BF16)
        h = _tail(o, wo, h, mod, g_ffn[i], wgu, wd)
        if need_ctx:
            if oc is None:
                hc = per_batch(_tail(None, None, flat(hc), modc, g_ffn[i], wgu, wd))
            else:
                hc = per_batch(_tail(flat(oc), wo, flat(hc), modc, g_ffn[i], wgu, wd))
    return h
```

```python
import functools

import numpy as np
import jax
import jax.numpy as jnp
from jax import lax
from jax.experimental import pallas as pl
from jax.experimental.pallas import tpu as pltpu

F32 = jnp.float32
BF16 = jnp.bfloat16

D = 1024
DEPTH = 4
GRID_W = 64
EPS = 1e-6
ROPE_THETA = 10000.0
NEG_INF = -1e30
HEAD_DIM = 64
D_FF = 2816
SWA_HEADS = 16
SWA_KV_HEADS = 4
SWA_GROUPS = SWA_HEADS // SWA_KV_HEADS
SWA_WINDOW = 128
NA_HEADS = 16
NA_KH = 8
NA_KW = 16
POOL_WINDOWS = (2, 4, 8, 16)
POOL_DG = D // len(POOL_WINDOWS)
MLA_HEADS = 16
MLA_NOPE = 64
MLA_ROPE = 32
MLA_V = 64
MLA_QK = MLA_NOPE + MLA_ROPE
MLA_Q_RANK = 256
MLA_KV_RANK = 128

LANES = 128
SUBLANES = 8
MXU_WIDTH = 256
PROJ_CHUNK = MXU_WIDTH
VMEM_LIMIT = 56 * 1024 * 1024

NA_QROWS = 4
NA_WROWS = 12
NA_QB = NA_QROWS * GRID_W
NA_WB = NA_WROWS * GRID_W


def _cparams(n_axes):
    return pltpu.CompilerParams(dimension_semantics=("parallel",) * n_axes, vmem_limit_bytes=VMEM_LIMIT)


LOG2E = 1.4426950408889634


def _const_spec(shape):
    nd = len(shape)
    return pl.BlockSpec(shape, lambda *_: (0,) * nd, pipeline_mode=pl.Buffered(1))


def _dot(a, b):
    return jnp.dot(a, b, preferred_element_type=F32)


def _dot_nt(a, b):
    return lax.dot_general(a, b, (((1,), (1,)), ((), ())), preferred_element_type=F32)


def _silu(x):
    return x / (1.0 + jnp.exp(-x))


def _lane_iota():
    return lax.broadcasted_iota(jnp.int32, (1, LANES), 1)


def _modnorm(h, g, shift, scale):
    ms = jnp.mean(h * h, axis=-1, keepdims=True)
    y = h * lax.rsqrt(ms + EPS) * g
    return y * (1.0 + scale) + shift


def _rms_full(x, g, n):
    ms = jnp.sum(x * x, axis=-1, keepdims=True) * (1.0 / n)
    return x * lax.rsqrt(ms + EPS) * g


def _rms_head64(y, g):
    lo = _lane_iota() < HEAD_DIM
    x2 = y * y
    sa = jnp.sum(jnp.where(lo, x2, 0.0), axis=-1, keepdims=True)
    sb = jnp.sum(jnp.where(lo, 0.0, x2), axis=-1, keepdims=True)
    ms = jnp.where(lo, sa, sb) * (1.0 / HEAD_DIM)
    return y * lax.rsqrt(ms + EPS) * g


def _rms_head64_mxu(y, g, ones_bd):
    x2 = y * y
    hi = x2.astype(BF16)
    lo = (x2 - hi.astype(F32)).astype(BF16)
    ms = (_dot(hi, ones_bd) + _dot(lo, ones_bd)) * (1.0 / HEAD_DIM)
    return y * lax.rsqrt(ms + EPS) * g


def _rope(y, cos, sin, half):
    first = (_lane_iota() % (2 * half)) < half
    sw = jnp.where(first, pltpu.roll(y, LANES - half, 1), pltpu.roll(y, half, 1))
    return y * cos + sw * sin


def _dup_halves(y):
    lo = _lane_iota() < HEAD_DIM
    yr = pltpu.roll(y, HEAD_DIM, 1)
    return jnp.where(lo, y, yr), jnp.where(lo, yr, y)


def _exp2_parts(parts, sink=None):
    m = None
    for s in parts:
        mi = jnp.max(s, axis=-1, keepdims=True)
        m = mi if m is None else jnp.maximum(m, mi)
    if sink is not None:
        m = jnp.maximum(m, sink)
    es = [jnp.exp2(s - m).astype(BF16) for s in parts]
    return es, (None if sink is None else jnp.exp2(sink - m))


def _pv(es, vals):
    o = None
    for e, v in zip(es, vals):
        oi = _dot(e, v)
        o = oi if o is None else o + oi
    return o


def _norm_lo(o, extra=None):
    den = pltpu.roll(o, HEAD_DIM, 1)
    return o / (den if extra is None else den + extra)


def _norm_hi(o, extra=None):
    return pltpu.roll(o, HEAD_DIM, 1) / (o if extra is None else o + extra)


def _pipelined(n, scores_fn, finish_fn):
    nxt = scores_fn(0)
    for i in range(n):
        cur = nxt
        if i + 1 < n:
            nxt = scores_fn(i + 1)
        finish_fn(i, cur)


def _with_ones(v):
    lo = _lane_iota() < HEAD_DIM
    vf = v.astype(F32)
    one = jnp.ones_like(vf)
    return (jnp.where(lo, vf, one).astype(BF16),
            jnp.where(lo, pltpu.roll(vf, HEAD_DIM, 1), one).astype(BF16))


def _ada_kernel(c_ref, w_ref, b_ref, o_ref):
    cond = _silu(c_ref[...]).astype(BF16)
    o_ref[0] = _dot(cond, w_ref[0].astype(BF16)) + b_ref[0]


def _ada(cvec, w_ada, b_ada):
    rows = cvec.shape[0]
    nblk = 6
    return pl.pallas_call(
        _ada_kernel,
        out_shape=jax.ShapeDtypeStruct((DEPTH, rows, 6 * D), F32),
        grid=(DEPTH, nblk),
        in_specs=[pl.BlockSpec((rows, D), lambda i, j: (0, 0)),
                  pl.BlockSpec((1, D, D), lambda i, j: (i, 0, j)),
                  pl.BlockSpec((1, 1, D), lambda i, j: (i, 0, j))],
        out_specs=pl.BlockSpec((1, rows, D), lambda i, j: (i, 0, j)),
        compiler_params=_cparams(2),
        name="ada",
    )(cvec, w_ada, b_ada.reshape(DEPTH, 1, 6 * D))


def _mod_spec(mod, batch_axis):
    if mod.shape[0] == 1:
        return pl.BlockSpec((1, 6, D), lambda *ids: (0, 0, 0))
    return pl.BlockSpec((1, 6, D), lambda *ids: (ids[batch_axis], 0, 0))


def _row_tile(L):
    return min(L, 512)


def _rope_table_specs(cos, sin, tm, rope):
    if rope:
        spec = pl.BlockSpec((tm, LANES), lambda b, i: (i, 0))
        return [spec, spec], [cos, sin]
    spec = pl.BlockSpec((tm, LANES), lambda b, i: (0, 0))
    return [spec, spec], [cos[:tm], sin[:tm]]


FFN_CHUNK = 256


def _tail_kernel(*refs, with_oproj):
    if with_oproj:
        o_ref, wo_ref, h_ref, mod_ref, g_ref, wgu_ref, wd_ref, out_ref, act_ref = refs
        h = h_ref[0] + mod_ref[0, 2:3, :] * _dot(o_ref[0], wo_ref[...])
    else:
        h_ref, mod_ref, g_ref, wgu_ref, wd_ref, out_ref, act_ref = refs
        h = h_ref[0]
    a = _modnorm(h, g_ref[...], mod_ref[0, 3:4, :], mod_ref[0, 4:5, :]).astype(BF16)
    for c in range(D_FF // FFN_CHUNK):
        lo = c * FFN_CHUNK
        gate = _dot(a, wgu_ref[:, lo:lo + FFN_CHUNK])
        up = _dot(a, wgu_ref[:, D_FF + lo:D_FF + lo + FFN_CHUNK])
        act_ref[:, lo:lo + FFN_CHUNK] = (_silu(gate) * up).astype(BF16)
    y = _dot(act_ref[...], wd_ref[...])
    out_ref[0] = h + mod_ref[0, 5:6, :] * y


def _tail(o, wo, h, mod, g, wgu, wd):
    B, L, _ = h.shape
    tm = _row_tile(L)
    row = pl.BlockSpec((1, tm, D), lambda b, i: (b, i, 0))
    with_oproj = o is not None
    specs = [row, _mod_spec(mod, 0), _const_spec((1, D)), _const_spec((D, 2 * D_FF)), _const_spec((D_FF, D))]
    args = [h, mod, g.reshape(1, D), wgu, wd]
    if with_oproj:
        specs = [row, _const_spec((D, D))] + specs
        args = [o, wo] + args
    return pl.pallas_call(
        functools.partial(_tail_kernel, with_oproj=with_oproj),
        out_shape=jax.ShapeDtypeStruct(h.shape, F32),
        grid=(B, L // tm),
        in_specs=specs,
        out_specs=row,
        scratch_shapes=[pltpu.VMEM((tm, D_FF), BF16)],
        compiler_params=_cparams(2),
        name="tail" if with_oproj else "ffn",
    )(*args)


def _axial_tables(L, half, lead, width):
    t = np.arange(L)
    freqs = ROPE_THETA ** (-np.arange(half, dtype=np.float64) / half)
    cos = np.ones((L, LANES), np.float64)
    sin = np.zeros((L, LANES), np.float64)
    for base in range(0, LANES, width):
        for axis, pos in enumerate((t // GRID_W, t % GRID_W)):
            ang = pos[:, None].astype(np.float64) * freqs[None, :]
            o = base + lead + axis * 2 * half
            cos[:, o:o + half] = np.cos(ang)
            cos[:, o + half:o + 2 * half] = np.cos(ang)
            sin[:, o:o + half] = -np.sin(ang)
            sin[:, o + half:o + 2 * half] = np.sin(ang)
    return jnp.asarray(cos, F32), jnp.asarray(sin, F32)


SWA_NQ = SWA_HEADS * HEAD_DIM
SWA_NK = SWA_KV_HEADS * HEAD_DIM
SWA_SCALE = HEAD_DIM ** -0.5 * LOG2E


def _swa_proj_kernel(h_ref, mod_ref, g_ref, w_ref, gq_ref, gk_ref, ones_ref, cos_ref, sin_ref,
                     q_ref, k_ref, v_ref, *, rope):
    a = _modnorm(h_ref[0], g_ref[...], mod_ref[0, 0:1, :], mod_ref[0, 1:2, :]).astype(BF16)
    ones_bd = ones_ref[...]
    lo = _lane_iota() < HEAD_DIM
    n_q = SWA_NQ // PROJ_CHUNK

    def normed(blk, gain):
        blk = _rms_head64_mxu(blk, gain, ones_bd)
        return _rope(blk, cos_ref[...], sin_ref[...], HEAD_DIM // 4) if rope else blk

    def matmul(c):
        return _dot(a, w_ref[:, c * PROJ_CHUNK:(c + 1) * PROJ_CHUNK])

    def epilogue(c, y):
        for half in range(PROJ_CHUNK // LANES):
            blk = y[:, half * LANES:(half + 1) * LANES]
            if c < n_q:
                cb = c * (PROJ_CHUNK // LANES) + half
                q_ref[0, :, cb * LANES:(cb + 1) * LANES] = normed(blk, gq_ref[...]).astype(BF16)
                continue
            out_ref = k_ref if c == n_q else v_ref
            if c == n_q:
                first, second = _dup_halves(normed(blk, gk_ref[...]))
            else:
                first, second = jnp.where(lo, blk, 1.0), jnp.where(lo, pltpu.roll(blk, HEAD_DIM, 1), 1.0)
            out_ref[0, :, (2 * half) * LANES:(2 * half + 1) * LANES] = first.astype(BF16)
            out_ref[0, :, (2 * half + 1) * LANES:(2 * half + 2) * LANES] = second.astype(BF16)

    _pipelined(n_q + 2, matmul, epilogue)


def _head_ones():
    half = np.arange(LANES) // HEAD_DIM
    return jnp.asarray(half[:, None] == half[None, :], BF16)


def _swa_proj(h, mod, g, w, gq, gk, cos, sin, rope):
    B, L, _ = h.shape
    tm = _row_tile(L)
    nkv = 2 * SWA_NK
    row = lambda b, i: (b, i, 0)
    table_specs, tables = _rope_table_specs(cos, sin, tm, rope)
    return pl.pallas_call(
        functools.partial(_swa_proj_kernel, rope=rope),
        out_shape=(jax.ShapeDtypeStruct((B, L, SWA_NQ), BF16),
                   jax.ShapeDtypeStruct((B, L, nkv), BF16),
                   jax.ShapeDtypeStruct((B, L, nkv), BF16)),
        grid=(B, L // tm),
        in_specs=[pl.BlockSpec((1, tm, D), row),
                  _mod_spec(mod, 0),
                  _const_spec((1, D)),
                  _const_spec((D, SWA_NQ + 2 * SWA_NK)),
                  _const_spec((1, LANES)),
                  _const_spec((1, LANES)),
                  _const_spec((LANES, LANES))] + table_specs,
        out_specs=(pl.BlockSpec((1, tm, SWA_NQ), row),
                   pl.BlockSpec((1, tm, nkv), row),
                   pl.BlockSpec((1, tm, nkv), row)),
        compiler_params=_cparams(2),
        name="swa_proj",
    )(h, mod, g.reshape(1, D), w, (jnp.tile(gq, 2) * SWA_SCALE).reshape(1, LANES),
      jnp.tile(gk, 2).reshape(1, LANES), _head_ones(), *tables)


SWA_QB = 256
SWA_SPAN = SWA_QB + 2 * SWA_WINDOW
SWA_STEP_BLOCKS = 4


def _swa_band_masks():
    iq = np.arange(SWA_QB)[:, None]
    ik = np.arange(SWA_SPAN)[None, :]
    masks = [np.where(np.abs(o * SWA_WINDOW + iq - ik) <= SWA_WINDOW, 0.0, NEG_INF) for o in range(3)]
    return jnp.asarray(np.stack(masks), F32)


def _swa_attn_kernel(*refs, with_window, L, nblocks):
    if with_window:
        q_ref, kd_ref, vd_ref, kc_ref, vc_ref, sink_ref, band_ref, o_ref = refs
    else:
        q_ref, kc_ref, vc_ref, sink_ref, o_ref = refs
    lo = _lane_iota() < HEAD_DIM

    def window_start(bi):
        start = (pl.program_id(1) * nblocks + bi) * SWA_QB
        w0 = pl.multiple_of(jnp.clip(start - SWA_WINDOW, 0, L - SWA_SPAN), SWA_WINDOW)
        return start, w0

    def keys_or_values(win_ref, ctx_ref, bi, cols):
        if not with_window:
            return ctx_ref[0, :, cols]
        _, w0 = window_start(bi)
        return jnp.concatenate([win_ref[0, pl.ds(w0, SWA_SPAN), cols], ctx_ref[0, :, cols]], axis=0)

    def scores(c):
        bi, hk = divmod(c, SWA_KV_HEADS)
        qrows = slice(bi * SWA_QB, (bi + 1) * SWA_QB)
        qs = []
        for g in range(SWA_GROUPS):
            cb = hk * 2 + g // 2
            qc = q_ref[0, qrows, cb * LANES:(cb + 1) * LANES]
            zero = jnp.zeros_like(qc)
            qs.append(jnp.where(lo, qc, zero) if g % 2 == 0 else jnp.where(lo, zero, qc))
        qstack = jnp.concatenate(qs, axis=0)
        cols = slice(hk * LANES, (hk + 1) * LANES)
        s = _dot_nt(qstack, keys_or_values(kd_ref if with_window else None, kc_ref, bi, cols))
        if with_window:
            start, w0 = window_start(bi)
            band = band_ref[(start - w0) // SWA_WINDOW]
            band = jnp.concatenate([band] * SWA_GROUPS, axis=0)
            s = jnp.concatenate([s[:, :SWA_SPAN] + band, s[:, SWA_SPAN:]], axis=1)
        return [s]

    def finish(c, parts):
        bi, hk = divmod(c, SWA_KV_HEADS)
        qrows = slice(bi * SWA_QB, (bi + 1) * SWA_QB)
        cols = slice(hk * LANES, (hk + 1) * LANES)
        vals = [keys_or_values(vd_ref if with_window else None, vc_ref, bi, cols)]
        es, sink_e = [], []
        for g in range(SWA_GROUPS):
            head_rows = slice(g * SWA_QB, (g + 1) * SWA_QB)
            eg, sg = _exp2_parts([s[head_rows] for s in parts], sink_ref[hk * SWA_GROUPS + g] * LOG2E)
            es.append(eg[0])
            sink_e.append(sg)
        o = _pv([jnp.concatenate(es, axis=0)], vals)
        for j in range(2):
            even = slice((2 * j) * SWA_QB, (2 * j + 1) * SWA_QB)
            odd = slice((2 * j + 1) * SWA_QB, (2 * j + 2) * SWA_QB)
            pair = jnp.where(lo, _norm_lo(o[even], sink_e[2 * j]), _norm_hi(o[odd], sink_e[2 * j + 1]))
            o_ref[0, qrows, (hk * 2 + j) * LANES:(hk * 2 + j + 1) * LANES] = pair.astype(BF16)

    _pipelined(nblocks * SWA_KV_HEADS, scores, finish)


def _swa_attn(q, kd, vd, kcd, vcd, sink):
    B, L, _ = q.shape
    Lc = kcd.shape[1]
    nkv = 2 * SWA_NK
    with_window = kd is not None
    nblocks = min(SWA_STEP_BLOCKS, L // SWA_QB)
    blk = lambda b, i: (b, i, 0)
    whole = lambda b, i: (b, 0, 0)
    specs = [pl.BlockSpec((1, nblocks * SWA_QB, SWA_NQ), blk)]
    args = [q]
    if with_window:
        specs += [pl.BlockSpec((1, L, nkv), whole), pl.BlockSpec((1, L, nkv), whole)]
        args += [kd, vd]
    specs += [pl.BlockSpec((1, Lc, nkv), whole), pl.BlockSpec((1, Lc, nkv), whole),
              pl.BlockSpec(memory_space=pltpu.SMEM)]
    args += [kcd, vcd, sink]
    if with_window:
        specs.append(_const_spec((3, SWA_QB, SWA_SPAN)))
        args.append(_swa_band_masks())
    return pl.pallas_call(
        functools.partial(_swa_attn_kernel, with_window=with_window, L=L, nblocks=nblocks),
        out_shape=jax.ShapeDtypeStruct((B, L, SWA_NQ), BF16),
        grid=(B, L // (nblocks * SWA_QB)),
        in_specs=specs,
        out_specs=pl.BlockSpec((1, nblocks * SWA_QB, SWA_NQ), blk),
        compiler_params=_cparams(2),
        name="swa_attn" if with_window else "swa_attn_ctx",
    )(*args)


NA_N = NA_HEADS * HEAD_DIM
NA_SCALE = HEAD_DIM ** -0.5 * LOG2E


def _na_proj_kernel(h_ref, mod_ref, g_ref, w_ref, gq_ref, gk_ref, q_ref, k_ref, v_ref):
    a = _modnorm(h_ref[0], g_ref[...], mod_ref[0, 0:1, :], mod_ref[0, 1:2, :]).astype(BF16)
    per_kind = NA_N // PROJ_CHUNK

    def matmul(c):
        return _dot(a, w_ref[:, c * PROJ_CHUNK:(c + 1) * PROJ_CHUNK])

    def epilogue(c, y):
        kind, cc = divmod(c, per_kind)
        for half in range(PROJ_CHUNK // LANES):
            blk = y[:, half * LANES:(half + 1) * LANES]
            cols = slice(cc * PROJ_CHUNK + half * LANES, cc * PROJ_CHUNK + (half + 1) * LANES)
            if kind == 0:
                q_ref[0, :, cols] = _rms_head64(blk, gq_ref[...]).astype(BF16)
            elif kind == 1:
                k_ref[0, :, cols] = _rms_head64(blk, gk_ref[...]).astype(BF16)
            else:
                v_ref[0, :, cols] = blk.astype(BF16)

    _pipelined(3 * per_kind, matmul, epilogue)


def _na_proj(h, mod, g, w, gq, gk):
    B, L, _ = h.shape
    tm = _row_tile(L)
    row = lambda b, i: (b, i, 0)
    out = jax.ShapeDtypeStruct((B, L, NA_N), BF16)
    return pl.pallas_call(
        _na_proj_kernel,
        out_shape=(out, out, out),
        grid=(B, L // tm),
        in_specs=[pl.BlockSpec((1, tm, D), row),
                  _mod_spec(mod, 0),
                  _const_spec((1, D)),
                  _const_spec((D, 3 * NA_N)),
                  _const_spec((1, LANES)),
                  _const_spec((1, LANES))],
        out_specs=(pl.BlockSpec((1, tm, NA_N), row),) * 3,
        compiler_params=_cparams(2),
        name="na_proj",
    )(h, mod, g.reshape(1, D), w, (jnp.tile(gq, 2) * NA_SCALE).reshape(1, LANES),
      jnp.tile(gk, 2).reshape(1, LANES))


NA_PAIRS = NA_WROWS // 2
NA_OFFSETS = 2 * NA_KH
NA_MASKED_KIND = (1, 0)


def _na_bias_tiles(rpb, kinds):
    H = rpb.shape[0]
    c = np.arange(GRID_W)
    col_start = np.clip(c - NA_KW // 2, 0, GRID_W - NA_KW)
    col_ok = (c[None, :] >= col_start[:, None]) & (c[None, :] < col_start[:, None] + NA_KW)
    dc = np.clip(c[None, :] - c[:, None], -(NA_KW - 1), NA_KW - 1) + NA_KW - 1
    onehot = (dc[None] == np.arange(2 * NA_KW - 1)[:, None, None]).astype(np.float32)
    e = jnp.einsum("hrd,dqk->hrqk", rpb.astype(F32) * LOG2E, jnp.asarray(onehot),
                   precision=lax.Precision.HIGHEST)
    e = jnp.where(jnp.asarray(col_ok), e, NEG_INF)
    masked = jnp.full((H, 1, GRID_W, GRID_W), NEG_INF, F32)
    ext = jnp.concatenate([masked, e, masked], axis=1)
    off = masked[:, 0]
    tiles = []
    for variant, shifted in kinds:
        first = ext[:, shifted] if variant != 2 else off
        second = ext[:, shifted + 1] if variant != 1 else off
        tiles.append(jnp.concatenate([first, second], axis=-1))
    return jnp.stack(tiles, axis=1)


def _na_tile_ids(rows):
    kh = min(NA_KH, rows)
    nblk = rows // NA_QROWS
    ids = np.zeros((nblk, NA_QROWS, NA_PAIRS), np.int32)
    kinds = []
    starts = []
    for g in range(nblk):
        ws = int(np.clip(NA_QROWS * g - NA_QROWS, 0, rows - NA_WROWS))
        starts.append(ws)
        for i in range(NA_QROWS):
            r = NA_QROWS * g + i
            r0 = int(np.clip(r - kh // 2, 0, rows - kh))
            for kp in range(NA_PAIRS):
                rk = ws + 2 * kp
                ok1 = r0 <= rk < r0 + kh
                ok2 = r0 <= rk + 1 < r0 + kh
                shifted = rk - r + NA_KH - 1 + 1
                if ok1 and ok2:
                    variant = 0
                elif ok1:
                    variant = 1
                elif ok2:
                    variant = 2
                else:
                    variant, shifted = NA_MASKED_KIND
                assert 0 <= shifted < NA_OFFSETS
                if (variant, shifted) not in kinds:
                    kinds.append((variant, shifted))
                ids[g, i, kp] = kinds.index((variant, shifted))
    return ids, kinds, starts


def _na_attn_kernel(*refs, with_lat, npairs, L):
    lo = _lane_iota() < HEAD_DIM
    if with_lat:
        q_ref, k_ref, v_ref, kc_ref, vc_ref, tile_ref, o_ref, v1_ref = refs
        ids, kinds, starts = _na_tile_ids(L // GRID_W)
        tq = NA_QB
    else:
        q_ref, kc_ref, vc_ref, o_ref = refs
        tq = L
    nblk = L // tq
    if with_lat:
        v1_ref[0], v1_ref[1] = _with_ones(v_ref[0])
    vc_ones = [_with_ones(vc_ref[0, :, p * LANES:(p + 1) * LANES]) for p in range(npairs)]

    def window(g):
        return slice(starts[g] * GRID_W, starts[g] * GRID_W + NA_WB)

    def scores(c):
        p, g = divmod(c, nblk)
        cols = slice(p * LANES, (p + 1) * LANES)
        q = q_ref[0, g * tq:(g + 1) * tq, cols]
        zero = jnp.zeros_like(q)
        qstack = jnp.concatenate([jnp.where(lo, q, zero), jnp.where(lo, zero, q)], axis=0)
        if not with_lat:
            return [_dot_nt(qstack, kc_ref[0, :, cols])]
        return [_dot_nt(qstack, jnp.concatenate([k_ref[0, window(g), cols], kc_ref[0, :, cols]], axis=0))]

    def window_softmax(g, s):
        rows = []
        for hh in range(2):
            for i in range(NA_QROWS):
                r0 = (hh * NA_QROWS + i) * GRID_W
                ctx = s[r0:r0 + GRID_W, NA_WB:]
                tiles = []
                for kp in range(NA_PAIRS):
                    slot = int(ids[g, i, kp])
                    if kinds[slot] == NA_MASKED_KIND:
                        tiles.append(None)
                    else:
                        tiles.append(s[r0:r0 + GRID_W, kp * LANES:(kp + 1) * LANES] + tile_ref[hh, slot])
                live = [t for t in tiles if t is not None] + [ctx[:, j * LANES:(j + 1) * LANES]
                                                              for j in range(ctx.shape[1] // LANES)]
                top = live[0]
                for t in live[1:]:
                    top = jnp.maximum(top, t)
                m = jnp.max(top, axis=-1, keepdims=True)
                zeros = jnp.zeros((GRID_W, LANES), BF16)
                nums = [zeros if t is None else jnp.exp2(t - m).astype(BF16) for t in tiles]
                rows.append(jnp.concatenate(nums + [jnp.exp2(ctx - m).astype(BF16)], axis=1))
        return jnp.concatenate(rows, axis=0)

    def finish(c, parts):
        p, g = divmod(c, nblk)
        es = [window_softmax(g, parts[0])] if with_lat else _exp2_parts(parts)[0]
        outs = []
        for hh in range(2):
            vals = vc_ones[p][hh]
            if with_lat:
                vals = jnp.concatenate([v1_ref[hh, window(g), :], vals], axis=0)
            outs.append(_pv([e[hh * tq:(hh + 1) * tq] for e in es], [vals]))
        pair = jnp.where(lo, _norm_lo(outs[0]), _norm_hi(outs[1]))
        o_ref[0, g * tq:(g + 1) * tq, p * LANES:(p + 1) * LANES] = pair.astype(BF16)

    _pipelined(npairs * nblk, scores, finish)


def _na_attn(q, k, v, kc, vc, tiles):
    B, L, _ = q.shape
    Lc = kc.shape[1]
    with_lat = k is not None
    scratch = []
    if with_lat:
        npairs = 1
        grid = (NA_HEADS // 2, B)
        blk = lambda n: pl.BlockSpec((1, n, LANES), lambda p, b: (b, 0, p))
        specs = [blk(L), blk(L), blk(L), blk(Lc), blk(Lc),
                 pl.BlockSpec((2,) + tiles.shape[1:], lambda p, b: (p, 0, 0, 0))]
        args = [q, k, v, kc, vc, tiles]
        scratch = [pltpu.VMEM((2, L, LANES), BF16)]
        out_spec = blk(L)
    else:
        npairs = NA_HEADS // 2
        grid = (B,)
        blk = lambda n: pl.BlockSpec((1, n, NA_N), lambda b: (b, 0, 0))
        specs = [blk(L), blk(Lc), blk(Lc)]
        args = [q, kc, vc]
        out_spec = blk(L)
    return pl.pallas_call(
        functools.partial(_na_attn_kernel, with_lat=with_lat, npairs=npairs, L=L),
        out_shape=jax.ShapeDtypeStruct((B, L, NA_N), BF16),
        grid=grid,
        in_specs=specs,
        out_specs=out_spec,
        scratch_shapes=scratch,
        compiler_params=_cparams(len(grid)),
        name="na_attn" if with_lat else "na_attn_ctx",
    )(*args)


POOL_HALO = SUBLANES


def _pool_bands(tm):
    t = np.arange(tm)[:, None]
    j = np.arange(tm + 2 * POOL_HALO)[None, :] - POOL_HALO
    return jnp.asarray(np.stack([(j >= t - w // 2) & (j < t - w // 2 + w) for w in POOL_WINDOWS]), BF16)


def _pool_kernel(h_ref, hp_ref, hn_ref, mod_ref, g_ref, band_ref, w_ref, b_ref, sc_ref, o_ref, *, tm, L):
    i = pl.program_id(1)
    nt = L // tm
    g = g_ref[...]
    shift, scale = mod_ref[0, 0:1, :], mod_ref[0, 1:2, :]
    h = h_ref[0]
    a = _modnorm(h, g, shift, scale)
    prev_on = (i > 0).astype(F32)
    next_on = (i < nt - 1).astype(F32)
    ext = jnp.concatenate([_modnorm(hp_ref[0], g, shift, scale) * prev_on, a,
                           _modnorm(hn_ref[0], g, shift, scale) * next_on], axis=0)
    ext_hi = ext.astype(BF16)
    ext_lo = (ext - ext_hi.astype(F32)).astype(BF16)
    t = i * tm + lax.broadcasted_iota(jnp.int32, (tm, 1), 0)
    group_cols = [slice(gi * POOL_DG, (gi + 1) * POOL_DG) for gi in range(len(POOL_WINDOWS))]
    sums = [_dot(band_ref[gi], ext_hi[:, cols]) + _dot(band_ref[gi], ext_lo[:, cols])
            for gi, cols in enumerate(group_cols)]
    pooled = []
    for gi, w in enumerate(POOL_WINDOWS):
        first = jnp.clip(t - w // 2, 0, L)
        last = jnp.clip(t - w // 2 + w, 0, L)
        cnt = (last - first).astype(F32)
        pooled.append((sums[gi] / cnt - a[:, group_cols[gi]]).astype(BF16))
    ys = [_dot(pooled[gi], w_ref[gi]) + b_ref[gi:gi + 1, :] for gi in range(len(POOL_WINDOWS))]
    y = jnp.concatenate(ys, axis=-1) * sc_ref[...]
    o_ref[0] = h + mod_ref[0, 2:3, :] * y


def _pool(h, mod, g, w, b, sc):
    B, L, _ = h.shape
    tm = min(L, 256)
    hb = tm // POOL_HALO
    last_halo = L // POOL_HALO - 1
    return pl.pallas_call(
        functools.partial(_pool_kernel, tm=tm, L=L),
        out_shape=jax.ShapeDtypeStruct(h.shape, F32),
        grid=(B, L // tm),
        in_specs=[pl.BlockSpec((1, tm, D), lambda b_, i: (b_, i, 0)),
                  pl.BlockSpec((1, POOL_HALO, D), lambda b_, i: (b_, jnp.maximum(i * hb - 1, 0), 0)),
                  pl.BlockSpec((1, POOL_HALO, D), lambda b_, i: (b_, jnp.minimum((i + 1) * hb, last_halo), 0)),
                  _mod_spec(mod, 0),
                  _const_spec((1, D)),
                  _const_spec((len(POOL_WINDOWS), tm, tm + 2 * POOL_HALO)),
                  _const_spec((len(POOL_WINDOWS), POOL_DG, POOL_DG)),
                  _const_spec((len(POOL_WINDOWS), POOL_DG)),
                  _const_spec((1, D))],
        out_specs=pl.BlockSpec((1, tm, D), lambda b_, i: (b_, i, 0)),
        compiler_params=_cparams(2),
        name="pool",
    )(h, h, h, mod, g.reshape(1, D), _pool_bands(tm), w, b, sc.reshape(1, D))


MLA_A_COLS = 512
MLA_HQ = MLA_HEADS * LANES
MLA_SCALE = MLA_QK ** -0.5 * LOG2E
MLA_X2_LANE = LANES - MLA_ROPE // 2
MLA_X1_LANE = MLA_X2_LANE - LANES // 2


def _mla_proj_kernel(*refs, rope, need_q):
    (h_ref, mod_ref, g_ref, wa_ref, gcq_ref, gckv_ref, wuq_ref, wuk_ref, wuv_ref,
     gq_ref, gk_ref, ones_ref, cos_ref, sin_ref) = refs[:14]
    outs = refs[14:]
    if need_q:
        q_ref, k_ref, v_ref = outs
    else:
        k_ref, v_ref = outs
    a = _modnorm(h_ref[0], g_ref[...], mod_ref[0, 0:1, :], mod_ref[0, 1:2, :]).astype(BF16)
    proj = _dot(a, wa_ref[...])
    def rotary(blk):
        return blk * cos_ref[...] + pltpu.roll(blk, LANES // 2, 1) * sin_ref[...] if rope else blk

    if need_q:
        cq = _rms_full(proj[:, :MLA_Q_RANK], gcq_ref[...], MLA_Q_RANK).astype(BF16)
        ones = ones_ref[...]
    ckv = _rms_full(proj[:, MLA_Q_RANK:MLA_Q_RANK + MLA_KV_RANK], gckv_ref[...], MLA_KV_RANK).astype(BF16)
    kr = proj[:, MLA_Q_RANK + MLA_KV_RANK:]
    kr_rot = rotary(kr * gk_ref[...])
    kr_ss = jnp.sum(kr * kr, axis=-1, keepdims=True)

    def q_head(blk):
        x2 = blk * blk
        hi = x2.astype(BF16)
        lo = (x2 - hi.astype(F32)).astype(BF16)
        ms = (_dot(hi, ones) + _dot(lo, ones)) * (1.0 / MLA_QK)
        return rotary(blk * lax.rsqrt(ms + EPS) * gq_ref[...])

    def k_head(blk):
        ms = (jnp.sum(blk * blk, axis=-1, keepdims=True) + kr_ss) * (1.0 / MLA_QK)
        return (blk * gk_ref[...] + kr_rot) * lax.rsqrt(ms + EPS)

    heads_per_chunk = PROJ_CHUNK // LANES
    chunks = []
    for j in range(MLA_HEADS // heads_per_chunk):
        chunks += ([("q", j)] if need_q else []) + [("k", j)]
    chunks += [("v", j) for j in range(MLA_HEADS * MLA_V // PROJ_CHUNK)]

    def matmul(c):
        kind, j = chunks[c]
        cols = slice(j * PROJ_CHUNK, (j + 1) * PROJ_CHUNK)
        if kind == "q":
            return _dot(cq, wuq_ref[:, cols])
        return _dot(ckv, wuk_ref[:, cols] if kind == "k" else wuv_ref[:, cols])

    def epilogue(c, y):
        kind, j = chunks[c]
        if kind == "v":
            v_ref[0, :, j * PROJ_CHUNK:(j + 1) * PROJ_CHUNK] = y.astype(BF16)
            return
        for half in range(heads_per_chunk):
            blk = y[:, half * LANES:(half + 1) * LANES]
            cols = slice((j * heads_per_chunk + half) * LANES, (j * heads_per_chunk + half + 1) * LANES)
            if kind == "q":
                q_ref[0, :, cols] = q_head(blk).astype(BF16)
            else:
                k_ref[0, :, cols] = k_head(blk).astype(BF16)

    _pipelined(len(chunks), matmul, epilogue)


def _mla_proj(h, mod, g, wts, cos, sin, rope, need_q):
    B, L, _ = h.shape
    tm = _row_tile(L)
    row = lambda b, i: (b, i, 0)
    out_shape = [jax.ShapeDtypeStruct((B, L, MLA_HQ), BF16), jax.ShapeDtypeStruct((B, L, MLA_HEADS * MLA_V), BF16)]
    out_specs = [pl.BlockSpec((1, tm, MLA_HQ), row), pl.BlockSpec((1, tm, MLA_HEADS * MLA_V), row)]
    if need_q:
        out_shape.insert(0, jax.ShapeDtypeStruct((B, L, MLA_HQ), BF16))
        out_specs.insert(0, pl.BlockSpec((1, tm, MLA_HQ), row))
    table_specs, tables = _rope_table_specs(cos, sin, tm, rope)
    return pl.pallas_call(
        functools.partial(_mla_proj_kernel, rope=rope, need_q=need_q),
        out_shape=tuple(out_shape),
        grid=(B, L // tm),
        in_specs=[pl.BlockSpec((1, tm, D), row),
                  _mod_spec(mod, 0),
                  _const_spec((1, D)),
                  _const_spec((D, MLA_A_COLS)),
                  _const_spec((1, MLA_Q_RANK)),
                  _const_spec((1, MLA_KV_RANK)),
                  _const_spec((MLA_Q_RANK, MLA_HQ)),
                  _const_spec((MLA_KV_RANK, MLA_HQ)),
                  _const_spec((MLA_KV_RANK, MLA_HEADS * MLA_V)),
                  _const_spec((1, LANES)),
                  _const_spec((1, LANES)),
                  _const_spec((LANES, LANES))] + table_specs,
        out_specs=tuple(out_specs),
        compiler_params=_cparams(2),
        name="mla_proj" if need_q else "mla_proj_kv",
    )(h, mod, g.reshape(1, D), wts["wa"], wts["gcq"], wts["gckv"], wts["wuq"], wts["wuk"], wts["wuv"],
      wts["gq"], wts["gk"], jnp.ones((LANES, LANES), BF16), *tables)


def _mla_head_lanes(nope, rope):
    q = MLA_ROPE // 4
    ref = nope if nope is not None else rope
    zeros = lambda n: jnp.zeros(ref.shape[:-1] + (n,), ref.dtype)
    n_a = MLA_X1_LANE
    n_b = MLA_NOPE - n_a
    pad = MLA_X2_LANE - (MLA_X1_LANE + 2 * q + n_b)
    nope_a, nope_b = (zeros(n_a), zeros(n_b)) if nope is None else (nope[..., :n_a], nope[..., n_a:])
    if rope is None:
        x1, x2 = zeros(2 * q), zeros(2 * q)
    else:
        x1 = jnp.concatenate([rope[..., 0:q], rope[..., 2 * q:3 * q]], axis=-1)
        x2 = jnp.concatenate([rope[..., q:2 * q], rope[..., 3 * q:4 * q]], axis=-1)
    return jnp.concatenate([nope_a, x1, nope_b, zeros(pad), x2], axis=-1)


def _mla_rope_tables(L):
    quarter = MLA_ROPE // 4
    t = np.arange(L)
    freqs = ROPE_THETA ** (-np.arange(quarter, dtype=np.float64) / quarter)
    cos = np.ones((L, LANES), np.float64)
    sin = np.zeros((L, LANES), np.float64)
    for axis, pos in enumerate((t // GRID_W, t % GRID_W)):
        ang = pos[:, None].astype(np.float64) * freqs[None, :]
        x1 = slice(MLA_X1_LANE + axis * quarter, MLA_X1_LANE + (axis + 1) * quarter)
        x2 = slice(MLA_X2_LANE + axis * quarter, MLA_X2_LANE + (axis + 1) * quarter)
        cos[:, x1] = np.cos(ang)
        cos[:, x2] = np.cos(ang)
        sin[:, x1] = -np.sin(ang)
        sin[:, x2] = np.sin(ang)
    return jnp.asarray(cos, F32), jnp.asarray(sin, F32)


def _mla_weights(w_a, g_cq, g_ckv, w_uq, w_ukv, g_q, g_k):
    H = MLA_HEADS
    spread = lambda t: _mla_head_lanes(t[..., :MLA_NOPE], t[..., MLA_NOPE:])
    n_lin = MLA_Q_RANK + MLA_KV_RANK
    wa = jnp.concatenate([w_a[:, :n_lin], _mla_head_lanes(None, w_a[:, n_lin:])], axis=1)
    wukv = w_ukv.reshape(MLA_KV_RANK, H, MLA_NOPE + MLA_V)
    wuk = _mla_head_lanes(wukv[:, :, :MLA_NOPE], None)
    return dict(wa=wa.astype(BF16),
                gcq=g_cq.reshape(1, MLA_Q_RANK), gckv=g_ckv.reshape(1, MLA_KV_RANK),
                wuq=spread(w_uq.reshape(MLA_Q_RANK, H, MLA_QK)).reshape(MLA_Q_RANK, MLA_HQ).astype(BF16),
                wuk=wuk.reshape(MLA_KV_RANK, MLA_HQ).astype(BF16),
                wuv=wukv[:, :, MLA_NOPE:].reshape(MLA_KV_RANK, H * MLA_V).astype(BF16),
                gq=(spread(g_q) * MLA_SCALE).reshape(1, LANES), gk=spread(g_k).reshape(1, LANES))


MLA_TQ = 512


def _mla_attn_kernel(q_ref, k_ref, kc_ref, v_ref, vc_ref, o_ref, kall_ref, v1_ref):
    lo = _lane_iota() < MLA_V
    L, Lc = k_ref.shape[1], kc_ref.shape[1]
    kall_ref[0:L, :] = k_ref[0]
    kall_ref[L:L + Lc, :] = kc_ref[0]
    v1_ref[0, 0:L, :], v1_ref[1, 0:L, :] = _with_ones(v_ref[0])
    v1_ref[0, L:L + Lc, :], v1_ref[1, L:L + Lc, :] = _with_ones(vc_ref[0])
    first_head = []

    def scores(c):
        qi, j = divmod(c, 2)
        cols = slice(j * LANES, (j + 1) * LANES)
        return [_dot_nt(q_ref[0, qi * MLA_TQ:(qi + 1) * MLA_TQ, cols], kall_ref[:, cols])]

    def finish(c, parts):
        qi, j = divmod(c, 2)
        es, _ = _exp2_parts(parts)
        o = _pv(es, [v1_ref[j]])
        if j == 0:
            first_head.append(_norm_lo(o))
        else:
            pair = jnp.where(lo, first_head.pop(), _norm_hi(o))
            o_ref[0, qi * MLA_TQ:(qi + 1) * MLA_TQ, :] = pair.astype(BF16)

    _pipelined(2 * (q_ref.shape[1] // MLA_TQ), scores, finish)


def _mla_attn(q, k, kc, v, vc):
    B, L, _ = q.shape
    Lc = kc.shape[1]
    npair = MLA_HEADS // 2
    return pl.pallas_call(
        _mla_attn_kernel,
        out_shape=jax.ShapeDtypeStruct((B, L, MLA_HEADS * MLA_V), BF16),
        grid=(B, npair),
        in_specs=[pl.BlockSpec((1, L, 2 * LANES), lambda b, p: (b, 0, p)),
                  pl.BlockSpec((1, L, 2 * LANES), lambda b, p: (b, 0, p)),
                  pl.BlockSpec((1, Lc, 2 * LANES), lambda b, p: (b, 0, p)),
                  pl.BlockSpec((1, L, LANES), lambda b, p: (b, 0, p)),
                  pl.BlockSpec((1, Lc, LANES), lambda b, p: (b, 0, p))],
        out_specs=pl.BlockSpec((1, L, LANES), lambda b, p: (b, 0, p)),
        scratch_shapes=[pltpu.VMEM((L + Lc, 2 * LANES), BF16), pltpu.VMEM((2, L + Lc, LANES), BF16)],
        compiler_params=_cparams(2),
        name="mla_attn",
    )(q, k, kc, v, vc)


def kernel(x, c, ctx, c_ctx, w_ada, b_ada, g_mix, g_ffn, w_gate_up, w_down,
           swa_w_qkv, swa_g_q, swa_g_k, swa_sink, swa_w_o,
           na_w_qkv, na_g_q, na_g_k, na_rpb, na_w_o,
           pool_w, pool_b, pool_scale,
           mla_w_a, mla_g_cq, mla_g_ckv, mla_w_uq, mla_w_ukv, mla_g_q, mla_g_k, mla_w_o):
    B, L, _ = x.shape
    n_rows = -(-(B + 1) // SUBLANES) * SUBLANES
    cvec = jnp.concatenate([c, c_ctx[None, :], jnp.zeros((n_rows - B - 1, D), F32)], axis=0)
    mods = _ada(cvec, w_ada, b_ada).reshape(DEPTH, n_rows, 6, D)

    cos_a, sin_a = _axial_tables(L, HEAD_DIM // 4, 0, HEAD_DIM)
    cos_d, sin_d = _mla_rope_tables(L)

    flat = lambda t: t.reshape(1, -1, t.shape[-1])
    per_batch = lambda t: t.reshape(B, -1, t.shape[-1])

    h, hc = x, ctx
    for i in range(DEPTH):
        m, j = i % 4, i // 4
        need_ctx = i < DEPTH - 1
        mod = mods[i, :B]
        modc = mods[i, B:B + 1]
        o = oc = wo = None
        if m == 0:
            w = swa_w_qkv[j].astype(BF16)
            wo = swa_w_o[j].astype(BF16)
            q, kd, vd = _swa_proj(h, mod, g_mix[i], w, swa_g_q[j], swa_g_k[j], cos_a, sin_a, True)
            qc, kcd, vcd = map(per_batch,
                               _swa_proj(flat(hc), modc, g_mix[i], w, swa_g_q[j], swa_g_k[j], cos_a, sin_a, False))
            o = _swa_attn(q, kd, vd, kcd, vcd, swa_sink[j])
            if need_ctx:
                oc = _swa_attn(qc, None, None, kcd, vcd, swa_sink[j])
        elif m == 1:
            w = na_w_qkv[j].astype(BF16)
            wo = na_w_o[j].astype(BF16)
            q, k, v = _na_proj(h, mod, g_mix[i], w, na_g_q[j], na_g_k[j])
            qc, kc, vc = map(per_batch, _na_proj(flat(hc), modc, g_mix[i], w, na_g_q[j], na_g_k[j]))
            o = _na_attn(q, k, v, kc, vc, _na_bias_tiles(na_rpb[j], _na_tile_ids(L // GRID_W)[1]))
            if need_ctx:
                oc = _na_attn(qc, None, None, kc, vc, None)
        elif m == 2:
            pw = pool_w[j].astype(BF16)
            h = _pool(h, mod, g_mix[i], pw, pool_b[j], pool_scale[j])
            if need_ctx:
                hc = _pool(hc, modc, g_mix[i], pw, pool_b[j], pool_scale[j])
        else:
            wts = _mla_weights(mla_w_a[j], mla_g_cq[j], mla_g_ckv[j], mla_w_uq[j], mla_w_ukv[j],
                               mla_g_q[j], mla_g_k[j])
            wo = mla_w_o[j].astype(BF16)
            assert not need_ctx, "the latent-attention mixer is only built for the last layer"
            q, k, v = _mla_proj(h, mod, g_mix[i], wts, cos_d, sin_d, True, True)
            kc, vc = map(per_batch, _mla_proj(flat(hc), modc, g_mix[i], wts, cos_d, sin_d, False, False))
            o = _mla_attn(q, k, kc, v, vc)
        wgu = w_gate_up[i].astype(BF16)
        wd = w_down[i].astype(BF16)
        h = _tail(o, wo, h, mod, g_ffn[i], wgu, wd)
        if need_ctx:
            if oc is None:
                hc = per_batch(_tail(None, None, flat(hc), modc, g_ffn[i], wgu, wd))
            else:
                hc = per_batch(_tail(flat(oc), wo, flat(hc), modc, g_ffn[i], wgu, wd))
    return h
```

```python
import functools

import numpy as np
import jax
import jax.numpy as jnp
from jax import lax
from jax.experimental import pallas as pl
from jax.experimental.pallas import tpu as pltpu

F32 = jnp.float32
BF16 = jnp.bfloat16

D = 1024
DEPTH = 4
GRID_W = 64
EPS = 1e-6
ROPE_THETA = 10000.0
NEG_INF = -1e30
HEAD_DIM = 64
D_FF = 2816
SWA_HEADS = 16
SWA_KV_HEADS = 4
SWA_GROUPS = SWA_HEADS // SWA_KV_HEADS
SWA_WINDOW = 128
NA_HEADS = 16
NA_KH = 8
NA_KW = 16
POOL_WINDOWS = (2, 4, 8, 16)
POOL_DG = D // len(POOL_WINDOWS)
MLA_HEADS = 16
MLA_NOPE = 64
MLA_ROPE = 32
MLA_V = 64
MLA_QK = MLA_NOPE + MLA_ROPE
MLA_Q_RANK = 256
MLA_KV_RANK = 128

LANES = 128
SUBLANES = 8
MXU_WIDTH = 256
PROJ_CHUNK = MXU_WIDTH
VMEM_LIMIT = 56 * 1024 * 1024

NA_QROWS = 4
NA_WROWS = 12
NA_QB = NA_QROWS * GRID_W
NA_WB = NA_WROWS * GRID_W


def _cparams(n_axes):
    return pltpu.CompilerParams(dimension_semantics=("parallel",) * n_axes, vmem_limit_bytes=VMEM_LIMIT)


LOG2E = 1.4426950408889634


def _const_spec(shape):
    nd = len(shape)
    return pl.BlockSpec(shape, lambda *_: (0,) * nd, pipeline_mode=pl.Buffered(1))


def _dot(a, b):
    return jnp.dot(a, b, preferred_element_type=F32)


def _dot_nt(a, b):
    return lax.dot_general(a, b, (((1,), (1,)), ((), ())), preferred_element_type=F32)


def _silu(x):
    return x / (1.0 + jnp.exp(-x))


def _lane_iota():
    return lax.broadcasted_iota(jnp.int32, (1, LANES), 1)


def _modnorm(h, g, shift, scale):
    ms = jnp.mean(h * h, axis=-1, keepdims=True)
    y = h * lax.rsqrt(ms + EPS) * g
    return y * (1.0 + scale) + shift


def _rms_full(x, g, n):
    ms = jnp.sum(x * x, axis=-1, keepdims=True) * (1.0 / n)
    return x * lax.rsqrt(ms + EPS) * g


def _rms_head64(y, g):
    lo = _lane_iota() < HEAD_DIM
    x2 = y * y
    sa = jnp.sum(jnp.where(lo, x2, 0.0), axis=-1, keepdims=True)
    sb = jnp.sum(jnp.where(lo, 0.0, x2), axis=-1, keepdims=True)
    ms = jnp.where(lo, sa, sb) * (1.0 / HEAD_DIM)
    return y * lax.rsqrt(ms + EPS) * g


def _rms_head64_mxu(y, g, ones_bd):
    x2 = y * y
    hi = x2.astype(BF16)
    lo = (x2 - hi.astype(F32)).astype(BF16)
    ms = (_dot(hi, ones_bd) + _dot(lo, ones_bd)) * (1.0 / HEAD_DIM)
    return y * lax.rsqrt(ms + EPS) * g


def _rope(y, cos, sin, half):
    first = (_lane_iota() % (2 * half)) < half
    sw = jnp.where(first, pltpu.roll(y, LANES - half, 1), pltpu.roll(y, half, 1))
    return y * cos + sw * sin


def _dup_halves(y):
    lo = _lane_iota() < HEAD_DIM
    yr = pltpu.roll(y, HEAD_DIM, 1)
    return jnp.where(lo, y, yr), jnp.where(lo, yr, y)


def _exp2_parts(parts, sink=None):
    m = None
    for s in parts:
        mi = jnp.max(s, axis=-1, keepdims=True)
        m = mi if m is None else jnp.maximum(m, mi)
    if sink is not None:
        m = jnp.maximum(m, sink)
    es = [jnp.exp2(s - m).astype(BF16) for s in parts]
    return es, (None if sink is None else jnp.exp2(sink - m))


def _pv(es, vals):
    o = None
    for e, v in zip(es, vals):
        oi = _dot(e, v)
        o = oi if o is None else o + oi
    return o


def _norm_lo(o, extra=None):
    den = pltpu.roll(o, HEAD_DIM, 1)
    return o / (den if extra is None else den + extra)


def _norm_hi(o, extra=None):
    return pltpu.roll(o, HEAD_DIM, 1) / (o if extra is None else o + extra)


def _pipelined(n, scores_fn, finish_fn, ahead=1):
    pending = [scores_fn(i) for i in range(min(ahead, n))]
    for i in range(n):
        if i + ahead < n:
            pending.append(scores_fn(i + ahead))
        finish_fn(i, pending.pop(0))


def _with_ones(v):
    lo = _lane_iota() < HEAD_DIM
    vf = v.astype(F32)
    one = jnp.ones_like(vf)
    return (jnp.where(lo, vf, one).astype(BF16),
            jnp.where(lo, pltpu.roll(vf, HEAD_DIM, 1), one).astype(BF16))


def _ada_kernel(c_ref, w_ref, b_ref, o_ref):
    cond = _silu(c_ref[...]).astype(BF16)
    o_ref[0] = _dot(cond, w_ref[0].astype(BF16)) + b_ref[0]


def _ada(cvec, w_ada, b_ada):
    rows = cvec.shape[0]
    nblk = 6
    return pl.pallas_call(
        _ada_kernel,
        out_shape=jax.ShapeDtypeStruct((DEPTH, rows, 6 * D), F32),
        grid=(DEPTH, nblk),
        in_specs=[pl.BlockSpec((rows, D), lambda i, j: (0, 0)),
                  pl.BlockSpec((1, D, D), lambda i, j: (i, 0, j)),
                  pl.BlockSpec((1, 1, D), lambda i, j: (i, 0, j))],
        out_specs=pl.BlockSpec((1, rows, D), lambda i, j: (i, 0, j)),
        compiler_params=_cparams(2),
        name="ada",
    )(cvec, w_ada, b_ada.reshape(DEPTH, 1, 6 * D))


def _mod_spec(mod, batch_axis):
    if mod.shape[0] == 1:
        return pl.BlockSpec((1, 6, D), lambda *ids: (0, 0, 0))
    return pl.BlockSpec((1, 6, D), lambda *ids: (ids[batch_axis], 0, 0))


def _row_tile(L):
    return min(L, 512)


def _rope_table_specs(cos, sin, tm, rope):
    if rope:
        spec = pl.BlockSpec((tm, LANES), lambda b, i: (i, 0))
        return [spec, spec], [cos, sin]
    spec = pl.BlockSpec((tm, LANES), lambda b, i: (0, 0))
    return [spec, spec], [cos[:tm], sin[:tm]]


FFN_CHUNK = 256


def _tail_kernel(*refs, with_oproj):
    if with_oproj:
        o_ref, wo_ref, h_ref, mod_ref, g_ref, wgu_ref, wd_ref, out_ref, act_ref = refs
        h = h_ref[0] + mod_ref[0, 2:3, :] * _dot(o_ref[0], wo_ref[...])
    else:
        h_ref, mod_ref, g_ref, wgu_ref, wd_ref, out_ref, act_ref = refs
        h = h_ref[0]
    a = _modnorm(h, g_ref[...], mod_ref[0, 3:4, :], mod_ref[0, 4:5, :]).astype(BF16)
    for c in range(D_FF // FFN_CHUNK):
        lo = c * FFN_CHUNK
        gate = _dot(a, wgu_ref[:, lo:lo + FFN_CHUNK])
        up = _dot(a, wgu_ref[:, D_FF + lo:D_FF + lo + FFN_CHUNK])
        act_ref[:, lo:lo + FFN_CHUNK] = (_silu(gate) * up).astype(BF16)
    y = _dot(act_ref[...], wd_ref[...])
    out_ref[0] = h + mod_ref[0, 5:6, :] * y


def _tail(o, wo, h, mod, g, wgu, wd):
    B, L, _ = h.shape
    tm = _row_tile(L)
    row = pl.BlockSpec((1, tm, D), lambda b, i: (b, i, 0))
    with_oproj = o is not None
    specs = [row, _mod_spec(mod, 0), _const_spec((1, D)), _const_spec((D, 2 * D_FF)), _const_spec((D_FF, D))]
    args = [h, mod, g.reshape(1, D), wgu, wd]
    if with_oproj:
        specs = [row, _const_spec((D, D))] + specs
        args = [o, wo] + args
    return pl.pallas_call(
        functools.partial(_tail_kernel, with_oproj=with_oproj),
        out_shape=jax.ShapeDtypeStruct(h.shape, F32),
        grid=(B, L // tm),
        in_specs=specs,
        out_specs=row,
        scratch_shapes=[pltpu.VMEM((tm, D_FF), BF16)],
        compiler_params=_cparams(2),
        name="tail" if with_oproj else "ffn",
    )(*args)


def _axial_tables(L, half, lead, width):
    t = np.arange(L)
    freqs = ROPE_THETA ** (-np.arange(half, dtype=np.float64) / half)
    cos = np.ones((L, LANES), np.float64)
    sin = np.zeros((L, LANES), np.float64)
    for base in range(0, LANES, width):
        for axis, pos in enumerate((t // GRID_W, t % GRID_W)):
            ang = pos[:, None].astype(np.float64) * freqs[None, :]
            o = base + lead + axis * 2 * half
            cos[:, o:o + half] = np.cos(ang)
            cos[:, o + half:o + 2 * half] = np.cos(ang)
            sin[:, o:o + half] = -np.sin(ang)
            sin[:, o + half:o + 2 * half] = np.sin(ang)
    return jnp.asarray(cos, F32), jnp.asarray(sin, F32)


SWA_NQ = SWA_HEADS * HEAD_DIM
SWA_NK = SWA_KV_HEADS * HEAD_DIM
SWA_SCALE = HEAD_DIM ** -0.5 * LOG2E


def _swa_proj_kernel(h_ref, mod_ref, g_ref, w_ref, gq_ref, gk_ref, ones_ref, cos_ref, sin_ref,
                     q_ref, k_ref, v_ref, *, rope):
    a = _modnorm(h_ref[0], g_ref[...], mod_ref[0, 0:1, :], mod_ref[0, 1:2, :]).astype(BF16)
    ones_bd = ones_ref[...]
    lo = _lane_iota() < HEAD_DIM
    n_q = SWA_NQ // PROJ_CHUNK

    def normed(blk, gain):
        blk = _rms_head64_mxu(blk, gain, ones_bd)
        return _rope(blk, cos_ref[...], sin_ref[...], HEAD_DIM // 4) if rope else blk

    def matmul(c):
        return _dot(a, w_ref[:, c * PROJ_CHUNK:(c + 1) * PROJ_CHUNK])

    def epilogue(c, y):
        for half in range(PROJ_CHUNK // LANES):
            blk = y[:, half * LANES:(half + 1) * LANES]
            if c < n_q:
                cb = c * (PROJ_CHUNK // LANES) + half
                q_ref[0, :, cb * LANES:(cb + 1) * LANES] = normed(blk, gq_ref[...]).astype(BF16)
                continue
            out_ref = k_ref if c == n_q else v_ref
            if c == n_q:
                first, second = _dup_halves(normed(blk, gk_ref[...]))
            else:
                first, second = jnp.where(lo, blk, 1.0), jnp.where(lo, pltpu.roll(blk, HEAD_DIM, 1), 1.0)
            out_ref[0, :, (2 * half) * LANES:(2 * half + 1) * LANES] = first.astype(BF16)
            out_ref[0, :, (2 * half + 1) * LANES:(2 * half + 2) * LANES] = second.astype(BF16)

    _pipelined(n_q + 2, matmul, epilogue)


def _head_ones():
    half = np.arange(LANES) // HEAD_DIM
    return jnp.asarray(half[:, None] == half[None, :], BF16)


def _swa_proj(h, mod, g, w, gq, gk, cos, sin, rope):
    B, L, _ = h.shape
    tm = _row_tile(L)
    nkv = 2 * SWA_NK
    row = lambda b, i: (b, i, 0)
    table_specs, tables = _rope_table_specs(cos, sin, tm, rope)
    return pl.pallas_call(
        functools.partial(_swa_proj_kernel, rope=rope),
        out_shape=(jax.ShapeDtypeStruct((B, L, SWA_NQ), BF16),
                   jax.ShapeDtypeStruct((B, L, nkv), BF16),
                   jax.ShapeDtypeStruct((B, L, nkv), BF16)),
        grid=(B, L // tm),
        in_specs=[pl.BlockSpec((1, tm, D), row),
                  _mod_spec(mod, 0),
                  _const_spec((1, D)),
                  _const_spec((D, SWA_NQ + 2 * SWA_NK)),
                  _const_spec((1, LANES)),
                  _const_spec((1, LANES)),
                  _const_spec((LANES, LANES))] + table_specs,
        out_specs=(pl.BlockSpec((1, tm, SWA_NQ), row),
                   pl.BlockSpec((1, tm, nkv), row),
                   pl.BlockSpec((1, tm, nkv), row)),
        compiler_params=_cparams(2),
        name="swa_proj",
    )(h, mod, g.reshape(1, D), w, (jnp.tile(gq, 2) * SWA_SCALE).reshape(1, LANES),
      jnp.tile(gk, 2).reshape(1, LANES), _head_ones(), *tables)


SWA_QB = 256
SWA_SPAN = SWA_QB + 2 * SWA_WINDOW
SWA_STEP_BLOCKS = 4


def _swa_band_masks():
    iq = np.arange(SWA_QB)[:, None]
    ik = np.arange(SWA_SPAN)[None, :]
    masks = [np.where(np.abs(o * SWA_WINDOW + iq - ik) <= SWA_WINDOW, 0.0, NEG_INF) for o in range(3)]
    return jnp.asarray(np.stack(masks), F32)


def _swa_attn_kernel(*refs, with_window, L, nblocks):
    if with_window:
        q_ref, kd_ref, vd_ref, kc_ref, vc_ref, sink_ref, band_ref, o_ref = refs
    else:
        q_ref, kc_ref, vc_ref, sink_ref, o_ref = refs
    lo = _lane_iota() < HEAD_DIM

    def window_start(bi):
        start = (pl.program_id(1) * nblocks + bi) * SWA_QB
        w0 = pl.multiple_of(jnp.clip(start - SWA_WINDOW, 0, L - SWA_SPAN), SWA_WINDOW)
        return start, w0

    def keys_or_values(win_ref, ctx_ref, bi, cols):
        if not with_window:
            return ctx_ref[0, :, cols]
        _, w0 = window_start(bi)
        return jnp.concatenate([win_ref[0, pl.ds(w0, SWA_SPAN), cols], ctx_ref[0, :, cols]], axis=0)

    def scores(c):
        bi, hk = divmod(c, SWA_KV_HEADS)
        qrows = slice(bi * SWA_QB, (bi + 1) * SWA_QB)
        qs = []
        for g in range(SWA_GROUPS):
            cb = hk * 2 + g // 2
            qc = q_ref[0, qrows, cb * LANES:(cb + 1) * LANES]
            zero = jnp.zeros_like(qc)
            qs.append(jnp.where(lo, qc, zero) if g % 2 == 0 else jnp.where(lo, zero, qc))
        qstack = jnp.concatenate(qs, axis=0)
        cols = slice(hk * LANES, (hk + 1) * LANES)
        s = _dot_nt(qstack, keys_or_values(kd_ref if with_window else None, kc_ref, bi, cols))
        if with_window:
            start, w0 = window_start(bi)
            band = band_ref[(start - w0) // SWA_WINDOW]
            band = jnp.concatenate([band] * SWA_GROUPS, axis=0)
            s = jnp.concatenate([s[:, :SWA_SPAN] + band, s[:, SWA_SPAN:]], axis=1)
        return [s]

    def finish(c, parts):
        bi, hk = divmod(c, SWA_KV_HEADS)
        qrows = slice(bi * SWA_QB, (bi + 1) * SWA_QB)
        cols = slice(hk * LANES, (hk + 1) * LANES)
        vals = [keys_or_values(vd_ref if with_window else None, vc_ref, bi, cols)]
        es, sink_e = [], []
        for g in range(SWA_GROUPS):
            head_rows = slice(g * SWA_QB, (g + 1) * SWA_QB)
            eg, sg = _exp2_parts([s[head_rows] for s in parts], sink_ref[hk * SWA_GROUPS + g] * LOG2E)
            es.append(eg[0])
            sink_e.append(sg)
        o = _pv([jnp.concatenate(es, axis=0)], vals)
        for j in range(2):
            even = slice((2 * j) * SWA_QB, (2 * j + 1) * SWA_QB)
            odd = slice((2 * j + 1) * SWA_QB, (2 * j + 2) * SWA_QB)
            pair = jnp.where(lo, _norm_lo(o[even], sink_e[2 * j]), _norm_hi(o[odd], sink_e[2 * j + 1]))
            o_ref[0, qrows, (hk * 2 + j) * LANES:(hk * 2 + j + 1) * LANES] = pair.astype(BF16)

    n_chains = nblocks * SWA_KV_HEADS
    _pipelined(n_chains, scores, finish, ahead=1 if with_window else n_chains)


def _swa_attn(q, kd, vd, kcd, vcd, sink):
    B, L, _ = q.shape
    Lc = kcd.shape[1]
    nkv = 2 * SWA_NK
    with_window = kd is not None
    nblocks = min(SWA_STEP_BLOCKS, L // SWA_QB)
    blk = lambda b, i: (b, i, 0)
    whole = lambda b, i: (b, 0, 0)
    specs = [pl.BlockSpec((1, nblocks * SWA_QB, SWA_NQ), blk)]
    args = [q]
    if with_window:
        specs += [pl.BlockSpec((1, L, nkv), whole), pl.BlockSpec((1, L, nkv), whole)]
        args += [kd, vd]
    specs += [pl.BlockSpec((1, Lc, nkv), whole), pl.BlockSpec((1, Lc, nkv), whole),
              pl.BlockSpec(memory_space=pltpu.SMEM)]
    args += [kcd, vcd, sink]
    if with_window:
        specs.append(_const_spec((3, SWA_QB, SWA_SPAN)))
        args.append(_swa_band_masks())
    return pl.pallas_call(
        functools.partial(_swa_attn_kernel, with_window=with_window, L=L, nblocks=nblocks),
        out_shape=jax.ShapeDtypeStruct((B, L, SWA_NQ), BF16),
        grid=(B, L // (nblocks * SWA_QB)),
        in_specs=specs,
        out_specs=pl.BlockSpec((1, nblocks * SWA_QB, SWA_NQ), blk),
        compiler_params=_cparams(2),
        name="swa_attn" if with_window else "swa_attn_ctx",
    )(*args)


NA_N = NA_HEADS * HEAD_DIM
NA_SCALE = HEAD_DIM ** -0.5 * LOG2E


def _na_proj_kernel(h_ref, mod_ref, g_ref, w_ref, gq_ref, gk_ref, q_ref, k_ref, v_ref):
    a = _modnorm(h_ref[0], g_ref[...], mod_ref[0, 0:1, :], mod_ref[0, 1:2, :]).astype(BF16)
    per_kind = NA_N // PROJ_CHUNK

    def matmul(c):
        return _dot(a, w_ref[:, c * PROJ_CHUNK:(c + 1) * PROJ_CHUNK])

    def epilogue(c, y):
        kind, cc = divmod(c, per_kind)
        for half in range(PROJ_CHUNK // LANES):
            blk = y[:, half * LANES:(half + 1) * LANES]
            cols = slice(cc * PROJ_CHUNK + half * LANES, cc * PROJ_CHUNK + (half + 1) * LANES)
            if kind == 0:
                q_ref[0, :, cols] = _rms_head64(blk, gq_ref[...]).astype(BF16)
            elif kind == 1:
                k_ref[0, :, cols] = _rms_head64(blk, gk_ref[...]).astype(BF16)
            else:
                v_ref[0, :, cols] = blk.astype(BF16)

    _pipelined(3 * per_kind, matmul, epilogue)


def _na_proj(h, mod, g, w, gq, gk):
    B, L, _ = h.shape
    tm = _row_tile(L)
    row = lambda b, i: (b, i, 0)
    out = jax.ShapeDtypeStruct((B, L, NA_N), BF16)
    return pl.pallas_call(
        _na_proj_kernel,
        out_shape=(out, out, out),
        grid=(B, L // tm),
        in_specs=[pl.BlockSpec((1, tm, D), row),
                  _mod_spec(mod, 0),
                  _const_spec((1, D)),
                  _const_spec((D, 3 * NA_N)),
                  _const_spec((1, LANES)),
                  _const_spec((1, LANES))],
        out_specs=(pl.BlockSpec((1, tm, NA_N), row),) * 3,
        compiler_params=_cparams(2),
        name="na_proj",
    )(h, mod, g.reshape(1, D), w, (jnp.tile(gq, 2) * NA_SCALE).reshape(1, LANES),
      jnp.tile(gk, 2).reshape(1, LANES))


NA_PAIRS = NA_WROWS // 2
NA_OFFSETS = 2 * NA_KH
NA_MASKED_KIND = (1, 0)


def _na_bias_tiles(rpb, kinds):
    H = rpb.shape[0]
    c = np.arange(GRID_W)
    col_start = np.clip(c - NA_KW // 2, 0, GRID_W - NA_KW)
    col_ok = (c[None, :] >= col_start[:, None]) & (c[None, :] < col_start[:, None] + NA_KW)
    dc = np.clip(c[None, :] - c[:, None], -(NA_KW - 1), NA_KW - 1) + NA_KW - 1
    onehot = (dc[None] == np.arange(2 * NA_KW - 1)[:, None, None]).astype(np.float32)
    e = jnp.einsum("hrd,dqk->hrqk", rpb.astype(F32) * LOG2E, jnp.asarray(onehot),
                   precision=lax.Precision.HIGHEST)
    e = jnp.where(jnp.asarray(col_ok), e, NEG_INF)
    masked = jnp.full((H, 1, GRID_W, GRID_W), NEG_INF, F32)
    ext = jnp.concatenate([masked, e, masked], axis=1)
    off = masked[:, 0]
    tiles = []
    for variant, shifted in kinds:
        first = ext[:, shifted] if variant != 2 else off
        second = ext[:, shifted + 1] if variant != 1 else off
        tiles.append(jnp.concatenate([first, second], axis=-1))
    return jnp.stack(tiles, axis=1)


def _na_tile_ids(rows):
    kh = min(NA_KH, rows)
    nblk = rows // NA_QROWS
    ids = np.zeros((nblk, NA_QROWS, NA_PAIRS), np.int32)
    kinds = []
    starts = []
    for g in range(nblk):
        ws = int(np.clip(NA_QROWS * g - NA_QROWS, 0, rows - NA_WROWS))
        starts.append(ws)
        for i in range(NA_QROWS):
            r = NA_QROWS * g + i
            r0 = int(np.clip(r - kh // 2, 0, rows - kh))
            for kp in range(NA_PAIRS):
                rk = ws + 2 * kp
                ok1 = r0 <= rk < r0 + kh
                ok2 = r0 <= rk + 1 < r0 + kh
                shifted = rk - r + NA_KH - 1 + 1
                if ok1 and ok2:
                    variant = 0
                elif ok1:
                    variant = 1
                elif ok2:
                    variant = 2
                else:
                    variant, shifted = NA_MASKED_KIND
                assert 0 <= shifted < NA_OFFSETS
                if (variant, shifted) not in kinds:
                    kinds.append((variant, shifted))
                ids[g, i, kp] = kinds.index((variant, shifted))
    return ids, kinds, starts


def _na_attn_kernel(*refs, with_lat, npairs, L):
    lo = _lane_iota() < HEAD_DIM
    if with_lat:
        q_ref, k_ref, v_ref, kc_ref, vc_ref, tile_ref, o_ref, v1_ref = refs
        ids, kinds, starts = _na_tile_ids(L // GRID_W)
        tq = NA_QB
    else:
        q_ref, kc_ref, vc_ref, o_ref = refs
        tq = L
    nblk = L // tq
    if with_lat:
        v1_ref[0], v1_ref[1] = _with_ones(v_ref[0])
    vc_ones = [_with_ones(vc_ref[0, :, p * LANES:(p + 1) * LANES]) for p in range(npairs)]

    def window(g):
        return slice(starts[g] * GRID_W, starts[g] * GRID_W + NA_WB)

    def scores(c):
        p, g = divmod(c, nblk)
        cols = slice(p * LANES, (p + 1) * LANES)
        q = q_ref[0, g * tq:(g + 1) * tq, cols]
        zero = jnp.zeros_like(q)
        qstack = jnp.concatenate([jnp.where(lo, q, zero), jnp.where(lo, zero, q)], axis=0)
        if not with_lat:
            return [_dot_nt(qstack, kc_ref[0, :, cols])]
        return [_dot_nt(qstack, jnp.concatenate([k_ref[0, window(g), cols], kc_ref[0, :, cols]], axis=0))]

    def window_softmax(g, s):
        rows = []
        for hh in range(2):
            for i in range(NA_QROWS):
                r0 = (hh * NA_QROWS + i) * GRID_W
                ctx = s[r0:r0 + GRID_W, NA_WB:]
                tiles = []
                for kp in range(NA_PAIRS):
                    slot = int(ids[g, i, kp])
                    if kinds[slot] == NA_MASKED_KIND:
                        tiles.append(None)
                    else:
                        tiles.append(s[r0:r0 + GRID_W, kp * LANES:(kp + 1) * LANES] + tile_ref[hh, slot])
                live = [t for t in tiles if t is not None] + [ctx[:, j * LANES:(j + 1) * LANES]
                                                              for j in range(ctx.shape[1] // LANES)]
                top = live[0]
                for t in live[1:]:
                    top = jnp.maximum(top, t)
                m = jnp.max(top, axis=-1, keepdims=True)
                zeros = jnp.zeros((GRID_W, LANES), BF16)
                nums = [zeros if t is None else jnp.exp2(t - m).astype(BF16) for t in tiles]
                rows.append(jnp.concatenate(nums + [jnp.exp2(ctx - m).astype(BF16)], axis=1))
        return jnp.concatenate(rows, axis=0)

    def finish(c, parts):
        p, g = divmod(c, nblk)
        es = [window_softmax(g, parts[0])] if with_lat else _exp2_parts(parts)[0]
        outs = []
        for hh in range(2):
            vals = vc_ones[p][hh]
            if with_lat:
                vals = jnp.concatenate([v1_ref[hh, window(g), :], vals], axis=0)
            outs.append(_pv([e[hh * tq:(hh + 1) * tq] for e in es], [vals]))
        pair = jnp.where(lo, _norm_lo(outs[0]), _norm_hi(outs[1]))
        o_ref[0, g * tq:(g + 1) * tq, p * LANES:(p + 1) * LANES] = pair.astype(BF16)

    _pipelined(npairs * nblk, scores, finish, ahead=1 if with_lat else npairs * nblk)


def _na_attn(q, k, v, kc, vc, tiles):
    B, L, _ = q.shape
    Lc = kc.shape[1]
    with_lat = k is not None
    scratch = []
    if with_lat:
        npairs = 1
        grid = (NA_HEADS // 2, B)
        blk = lambda n: pl.BlockSpec((1, n, LANES), lambda p, b: (b, 0, p))
        specs = [blk(L), blk(L), blk(L), blk(Lc), blk(Lc),
                 pl.BlockSpec((2,) + tiles.shape[1:], lambda p, b: (p, 0, 0, 0))]
        args = [q, k, v, kc, vc, tiles]
        scratch = [pltpu.VMEM((2, L, LANES), BF16)]
        out_spec = blk(L)
    else:
        npairs = NA_HEADS // 2
        grid = (B,)
        blk = lambda n: pl.BlockSpec((1, n, NA_N), lambda b: (b, 0, 0))
        specs = [blk(L), blk(Lc), blk(Lc)]
        args = [q, kc, vc]
        out_spec = blk(L)
    return pl.pallas_call(
        functools.partial(_na_attn_kernel, with_lat=with_lat, npairs=npairs, L=L),
        out_shape=jax.ShapeDtypeStruct((B, L, NA_N), BF16),
        grid=grid,
        in_specs=specs,
        out_specs=out_spec,
        scratch_shapes=scratch,
        compiler_params=_cparams(len(grid)),
        name="na_attn" if with_lat else "na_attn_ctx",
    )(*args)


POOL_HALO = SUBLANES


def _pool_bands(tm):
    t = np.arange(tm)[:, None]
    j = np.arange(tm + 2 * POOL_HALO)[None, :] - POOL_HALO
    return jnp.asarray(np.stack([(j >= t - w // 2) & (j < t - w // 2 + w) for w in POOL_WINDOWS]), BF16)


def _pool_kernel(h_ref, hp_ref, hn_ref, mod_ref, g_ref, band_ref, w_ref, b_ref, sc_ref, o_ref, *, tm, L):
    i = pl.program_id(1)
    nt = L // tm
    g = g_ref[...]
    shift, scale = mod_ref[0, 0:1, :], mod_ref[0, 1:2, :]
    h = h_ref[0]
    a = _modnorm(h, g, shift, scale)
    prev_on = (i > 0).astype(F32)
    next_on = (i < nt - 1).astype(F32)
    ext = jnp.concatenate([_modnorm(hp_ref[0], g, shift, scale) * prev_on, a,
                           _modnorm(hn_ref[0], g, shift, scale) * next_on], axis=0)
    ext_hi = ext.astype(BF16)
    ext_lo = (ext - ext_hi.astype(F32)).astype(BF16)
    t = i * tm + lax.broadcasted_iota(jnp.int32, (tm, 1), 0)
    group_cols = [slice(gi * POOL_DG, (gi + 1) * POOL_DG) for gi in range(len(POOL_WINDOWS))]
    sums = [_dot(band_ref[gi], ext_hi[:, cols]) + _dot(band_ref[gi], ext_lo[:, cols])
            for gi, cols in enumerate(group_cols)]
    pooled = []
    for gi, w in enumerate(POOL_WINDOWS):
        first = jnp.clip(t - w // 2, 0, L)
        last = jnp.clip(t - w // 2 + w, 0, L)
        cnt = (last - first).astype(F32)
        pooled.append((sums[gi] / cnt - a[:, group_cols[gi]]).astype(BF16))
    ys = [_dot(pooled[gi], w_ref[gi]) + b_ref[gi:gi + 1, :] for gi in range(len(POOL_WINDOWS))]
    y = jnp.concatenate(ys, axis=-1) * sc_ref[...]
    o_ref[0] = h + mod_ref[0, 2:3, :] * y


def _pool(h, mod, g, w, b, sc):
    B, L, _ = h.shape
    tm = min(L, 256)
    hb = tm // POOL_HALO
    last_halo = L // POOL_HALO - 1
    return pl.pallas_call(
        functools.partial(_pool_kernel, tm=tm, L=L),
        out_shape=jax.ShapeDtypeStruct(h.shape, F32),
        grid=(B, L // tm),
        in_specs=[pl.BlockSpec((1, tm, D), lambda b_, i: (b_, i, 0)),
                  pl.BlockSpec((1, POOL_HALO, D), lambda b_, i: (b_, jnp.maximum(i * hb - 1, 0), 0)),
                  pl.BlockSpec((1, POOL_HALO, D), lambda b_, i: (b_, jnp.minimum((i + 1) * hb, last_halo), 0)),
                  _mod_spec(mod, 0),
                  _const_spec((1, D)),
                  _const_spec((len(POOL_WINDOWS), tm, tm + 2 * POOL_HALO)),
                  _const_spec((len(POOL_WINDOWS), POOL_DG, POOL_DG)),
                  _const_spec((len(POOL_WINDOWS), POOL_DG)),
                  _const_spec((1, D))],
        out_specs=pl.BlockSpec((1, tm, D), lambda b_, i: (b_, i, 0)),
        compiler_params=_cparams(2),
        name="pool",
    )(h, h, h, mod, g.reshape(1, D), _pool_bands(tm), w, b, sc.reshape(1, D))


MLA_A_COLS = 512
MLA_HQ = MLA_HEADS * LANES
MLA_SCALE = MLA_QK ** -0.5 * LOG2E
MLA_X2_LANE = LANES - MLA_ROPE // 2
MLA_X1_LANE = MLA_X2_LANE - LANES // 2


def _mla_proj_kernel(*refs, rope, need_q):
    (h_ref, mod_ref, g_ref, wa_ref, gcq_ref, gckv_ref, wuq_ref, wuk_ref, wuv_ref,
     gq_ref, gk_ref, ones_ref, cos_ref, sin_ref) = refs[:14]
    outs = refs[14:]
    if need_q:
        q_ref, k_ref, v_ref = outs
    else:
        k_ref, v_ref = outs
    a = _modnorm(h_ref[0], g_ref[...], mod_ref[0, 0:1, :], mod_ref[0, 1:2, :]).astype(BF16)
    proj = _dot(a, wa_ref[...])
    def rotary(blk):
        return blk * cos_ref[...] + pltpu.roll(blk, LANES // 2, 1) * sin_ref[...] if rope else blk

    if need_q:
        cq = _rms_full(proj[:, :MLA_Q_RANK], gcq_ref[...], MLA_Q_RANK).astype(BF16)
        ones = ones_ref[...]
    ckv = _rms_full(proj[:, MLA_Q_RANK:MLA_Q_RANK + MLA_KV_RANK], gckv_ref[...], MLA_KV_RANK).astype(BF16)
    kr = proj[:, MLA_Q_RANK + MLA_KV_RANK:]
    kr_rot = rotary(kr * gk_ref[...])
    kr_ss = jnp.sum(kr * kr, axis=-1, keepdims=True)

    def q_head(blk):
        x2 = blk * blk
        hi = x2.astype(BF16)
        lo = (x2 - hi.astype(F32)).astype(BF16)
        ms = (_dot(hi, ones) + _dot(lo, ones)) * (1.0 / MLA_QK)
        return rotary(blk * lax.rsqrt(ms + EPS) * gq_ref[...])

    def k_head(blk):
        ms = (jnp.sum(blk * blk, axis=-1, keepdims=True) + kr_ss) * (1.0 / MLA_QK)
        return (blk * gk_ref[...] + kr_rot) * lax.rsqrt(ms + EPS)

    heads_per_chunk = PROJ_CHUNK // LANES
    chunks = []
    for j in range(MLA_HEADS // heads_per_chunk):
        chunks += ([("q", j)] if need_q else []) + [("k", j)]
    chunks += [("v", j) for j in range(MLA_HEADS * MLA_V // PROJ_CHUNK)]

    def matmul(c):
        kind, j = chunks[c]
        cols = slice(j * PROJ_CHUNK, (j + 1) * PROJ_CHUNK)
        if kind == "q":
            return _dot(cq, wuq_ref[:, cols])
        return _dot(ckv, wuk_ref[:, cols] if kind == "k" else wuv_ref[:, cols])

    def epilogue(c, y):
        kind, j = chunks[c]
        if kind == "v":
            v_ref[0, :, j * PROJ_CHUNK:(j + 1) * PROJ_CHUNK] = y.astype(BF16)
            return
        for half in range(heads_per_chunk):
            blk = y[:, half * LANES:(half + 1) * LANES]
            cols = slice((j * heads_per_chunk + half) * LANES, (j * heads_per_chunk + half + 1) * LANES)
            if kind == "q":
                q_ref[0, :, cols] = q_head(blk).astype(BF16)
            else:
                k_ref[0, :, cols] = k_head(blk).astype(BF16)

    _pipelined(len(chunks), matmul, epilogue)


def _mla_proj(h, mod, g, wts, cos, sin, rope, need_q):
    B, L, _ = h.shape
    tm = _row_tile(L)
    row = lambda b, i: (b, i, 0)
    out_shape = [jax.ShapeDtypeStruct((B, L, MLA_HQ), BF16), jax.ShapeDtypeStruct((B, L, MLA_HEADS * MLA_V), BF16)]
    out_specs = [pl.BlockSpec((1, tm, MLA_HQ), row), pl.BlockSpec((1, tm, MLA_HEADS * MLA_V), row)]
    if need_q:
        out_shape.insert(0, jax.ShapeDtypeStruct((B, L, MLA_HQ), BF16))
        out_specs.insert(0, pl.BlockSpec((1, tm, MLA_HQ), row))
    table_specs, tables = _rope_table_specs(cos, sin, tm, rope)
    return pl.pallas_call(
        functools.partial(_mla_proj_kernel, rope=rope, need_q=need_q),
        out_shape=tuple(out_shape),
        grid=(B, L // tm),
        in_specs=[pl.BlockSpec((1, tm, D), row),
                  _mod_spec(mod, 0),
                  _const_spec((1, D)),
                  _const_spec((D, MLA_A_COLS)),
                  _const_spec((1, MLA_Q_RANK)),
                  _const_spec((1, MLA_KV_RANK)),
                  _const_spec((MLA_Q_RANK, MLA_HQ)),
                  _const_spec((MLA_KV_RANK, MLA_HQ)),
                  _const_spec((MLA_KV_RANK, MLA_HEADS * MLA_V)),
                  _const_spec((1, LANES)),
                  _const_spec((1, LANES)),
                  _const_spec((LANES, LANES))] + table_specs,
        out_specs=tuple(out_specs),
        compiler_params=_cparams(2),
        name="mla_proj" if need_q else "mla_proj_kv",
    )(h, mod, g.reshape(1, D), wts["wa"], wts["gcq"], wts["gckv"], wts["wuq"], wts["wuk"], wts["wuv"],
      wts["gq"], wts["gk"], jnp.ones((LANES, LANES), BF16), *tables)


def _mla_head_lanes(nope, rope):
    q = MLA_ROPE // 4
    ref = nope if nope is not None else rope
    zeros = lambda n: jnp.zeros(ref.shape[:-1] + (n,), ref.dtype)
    n_a = MLA_X1_LANE
    n_b = MLA_NOPE - n_a
    pad = MLA_X2_LANE - (MLA_X1_LANE + 2 * q + n_b)
    nope_a, nope_b = (zeros(n_a), zeros(n_b)) if nope is None else (nope[..., :n_a], nope[..., n_a:])
    if rope is None:
        x1, x2 = zeros(2 * q), zeros(2 * q)
    else:
        x1 = jnp.concatenate([rope[..., 0:q], rope[..., 2 * q:3 * q]], axis=-1)
        x2 = jnp.concatenate([rope[..., q:2 * q], rope[..., 3 * q:4 * q]], axis=-1)
    return jnp.concatenate([nope_a, x1, nope_b, zeros(pad), x2], axis=-1)


def _mla_rope_tables(L):
    quarter = MLA_ROPE // 4
    t = np.arange(L)
    freqs = ROPE_THETA ** (-np.arange(quarter, dtype=np.float64) / quarter)
    cos = np.ones((L, LANES), np.float64)
    sin = np.zeros((L, LANES), np.float64)
    for axis, pos in enumerate((t // GRID_W, t % GRID_W)):
        ang = pos[:, None].astype(np.float64) * freqs[None, :]
        x1 = slice(MLA_X1_LANE + axis * quarter, MLA_X1_LANE + (axis + 1) * quarter)
        x2 = slice(MLA_X2_LANE + axis * quarter, MLA_X2_LANE + (axis + 1) * quarter)
        cos[:, x1] = np.cos(ang)
        cos[:, x2] = np.cos(ang)
        sin[:, x1] = -np.sin(ang)
        sin[:, x2] = np.sin(ang)
    return jnp.asarray(cos, F32), jnp.asarray(sin, F32)


def _mla_weights(w_a, g_cq, g_ckv, w_uq, w_ukv, g_q, g_k):
    H = MLA_HEADS
    spread = lambda t: _mla_head_lanes(t[..., :MLA_NOPE], t[..., MLA_NOPE:])
    n_lin = MLA_Q_RANK + MLA_KV_RANK
    wa = jnp.concatenate([w_a[:, :n_lin], _mla_head_lanes(None, w_a[:, n_lin:])], axis=1)
    wukv = w_ukv.reshape(MLA_KV_RANK, H, MLA_NOPE + MLA_V)
    wuk = _mla_head_lanes(wukv[:, :, :MLA_NOPE], None)
    return dict(wa=wa.astype(BF16),
                gcq=g_cq.reshape(1, MLA_Q_RANK), gckv=g_ckv.reshape(1, MLA_KV_RANK),
                wuq=spread(w_uq.reshape(MLA_Q_RANK, H, MLA_QK)).reshape(MLA_Q_RANK, MLA_HQ).astype(BF16),
                wuk=wuk.reshape(MLA_KV_RANK, MLA_HQ).astype(BF16),
                wuv=wukv[:, :, MLA_NOPE:].reshape(MLA_KV_RANK, H * MLA_V).astype(BF16),
                gq=(spread(g_q) * MLA_SCALE).reshape(1, LANES), gk=spread(g_k).reshape(1, LANES))


MLA_TQ = 512


def _mla_attn_kernel(q_ref, k_ref, kc_ref, v_ref, vc_ref, o_ref, kall_ref, v1_ref):
    lo = _lane_iota() < MLA_V
    L, Lc = k_ref.shape[1], kc_ref.shape[1]
    kall_ref[0:L, :] = k_ref[0]
    kall_ref[L:L + Lc, :] = kc_ref[0]
    v1_ref[0, 0:L, :], v1_ref[1, 0:L, :] = _with_ones(v_ref[0])
    v1_ref[0, L:L + Lc, :], v1_ref[1, L:L + Lc, :] = _with_ones(vc_ref[0])
    first_head = []

    def scores(c):
        qi, j = divmod(c, 2)
        cols = slice(j * LANES, (j + 1) * LANES)
        return [_dot_nt(q_ref[0, qi * MLA_TQ:(qi + 1) * MLA_TQ, cols], kall_ref[:, cols])]

    def finish(c, parts):
        qi, j = divmod(c, 2)
        es, _ = _exp2_parts(parts)
        o = _pv(es, [v1_ref[j]])
        if j == 0:
            first_head.append(_norm_lo(o))
        else:
            pair = jnp.where(lo, first_head.pop(), _norm_hi(o))
            o_ref[0, qi * MLA_TQ:(qi + 1) * MLA_TQ, :] = pair.astype(BF16)

    _pipelined(2 * (q_ref.shape[1] // MLA_TQ), scores, finish)


def _mla_attn(q, k, kc, v, vc):
    B, L, _ = q.shape
    Lc = kc.shape[1]
    npair = MLA_HEADS // 2
    return pl.pallas_call(
        _mla_attn_kernel,
        out_shape=jax.ShapeDtypeStruct((B, L, MLA_HEADS * MLA_V), BF16),
        grid=(B, npair),
        in_specs=[pl.BlockSpec((1, L, 2 * LANES), lambda b, p: (b, 0, p)),
                  pl.BlockSpec((1, L, 2 * LANES), lambda b, p: (b, 0, p)),
                  pl.BlockSpec((1, Lc, 2 * LANES), lambda b, p: (b, 0, p)),
                  pl.BlockSpec((1, L, LANES), lambda b, p: (b, 0, p)),
                  pl.BlockSpec((1, Lc, LANES), lambda b, p: (b, 0, p))],
        out_specs=pl.BlockSpec((1, L, LANES), lambda b, p: (b, 0, p)),
        scratch_shapes=[pltpu.VMEM((L + Lc, 2 * LANES), BF16), pltpu.VMEM((2, L + Lc, LANES), BF16)],
        compiler_params=_cparams(2),
        name="mla_attn",
    )(q, k, kc, v, vc)


def kernel(x, c, ctx, c_ctx, w_ada, b_ada, g_mix, g_ffn, w_gate_up, w_down,
           swa_w_qkv, swa_g_q, swa_g_k, swa_sink, swa_w_o,
           na_w_qkv, na_g_q, na_g_k, na_rpb, na_w_o,
           pool_w, pool_b, pool_scale,
           mla_w_a, mla_g_cq, mla_g_ckv, mla_w_uq, mla_w_ukv, mla_g_q, mla_g_k, mla_w_o):
    B, L, _ = x.shape
    n_rows = -(-(B + 1) // SUBLANES) * SUBLANES
    cvec = jnp.concatenate([c, c_ctx[None, :], jnp.zeros((n_rows - B - 1, D), F32)], axis=0)
    mods = _ada(cvec, w_ada, b_ada).reshape(DEPTH, n_rows, 6, D)

    cos_a, sin_a = _axial_tables(L, HEAD_DIM // 4, 0, HEAD_DIM)
    cos_d, sin_d = _mla_rope_tables(L)

    flat = lambda t: t.reshape(1, -1, t.shape[-1])
    per_batch = lambda t: t.reshape(B, -1, t.shape[-1])

    h, hc = x, ctx
    for i in range(DEPTH):
        m, j = i % 4, i // 4
        need_ctx = i < DEPTH - 1
        mod = mods[i, :B]
        modc = mods[i, B:B + 1]
        o = oc = wo = None
        if m == 0:
            w = swa_w_qkv[j].astype(BF16)
            wo = swa_w_o[j].astype(BF16)
            q, kd, vd = _swa_proj(h, mod, g_mix[i], w, swa_g_q[j], swa_g_k[j], cos_a, sin_a, True)
            qc, kcd, vcd = map(per_batch,
                               _swa_proj(flat(hc), modc, g_mix[i], w, swa_g_q[j], swa_g_k[j], cos_a, sin_a, False))
            o = _swa_attn(q, kd, vd, kcd, vcd, swa_sink[j])
            if need_ctx:
                oc = _swa_attn(qc, None, None, kcd, vcd, swa_sink[j])
        elif m == 1:
            w = na_w_qkv[j].astype(BF16)
            wo = na_w_o[j].astype(BF16)
            q, k, v = _na_proj(h, mod, g_mix[i], w, na_g_q[j], na_g_k[j])
            qc, kc, vc = map(per_batch, _na_proj(flat(hc), modc, g_mix[i], w, na_g_q[j], na_g_k[j]))
            o = _na_attn(q, k, v, kc, vc, _na_bias_tiles(na_rpb[j], _na_tile_ids(L // GRID_W)[1]))
            if need_ctx:
                oc = _na_attn(qc, None, None, kc, vc, None)
        elif m == 2:
            pw = pool_w[j].astype(BF16)
            h = _pool(h, mod, g_mix[i], pw, pool_b[j], pool_scale[j])
            if need_ctx:
                hc = _pool(hc, modc, g_mix[i], pw, pool_b[j], pool_scale[j])
        else:
            wts = _mla_weights(mla_w_a[j], mla_g_cq[j], mla_g_ckv[j], mla_w_uq[j], mla_w_ukv[j],
                               mla_g_q[j], mla_g_k[j])
            wo = mla_w_o[j].astype(BF16)
            assert not need_ctx, "the latent-attention mixer is only built for the last layer"
            q, k, v = _mla_proj(h, mod, g_mix[i], wts, cos_d, sin_d, True, True)
            kc, vc = map(per_batch, _mla_proj(flat(hc), modc, g_mix[i], wts, cos_d, sin_d, False, False))
            o = _mla_attn(q, k, kc, v, vc)
        wgu = w_gate_up[i].astype(BF16)
        wd = w_down[i].astype(BF16)
        h = _tail(o, wo, h, mod, g_ffn[i], wgu, wd)
        if need_ctx:
            if oc is None:
                hc = per_batch(_tail(None, None, flat(hc), modc, g_ffn[i], wgu, wd))
            else:
                hc = per_batch(_tail(flat(oc), wo, flat(hc), modc, g_ffn[i], wgu, wd))
    return h
```

```python
import functools

import numpy as np
import jax
import jax.numpy as jnp
from jax import lax
from jax.experimental import pallas as pl
from jax.experimental.pallas import tpu as pltpu

F32 = jnp.float32
BF16 = jnp.bfloat16

D = 1024
DEPTH = 4
GRID_W = 64
EPS = 1e-6
ROPE_THETA = 10000.0
NEG_INF = -1e30
HEAD_DIM = 64
D_FF = 2816
SWA_HEADS = 16
SWA_KV_HEADS = 4
SWA_GROUPS = SWA_HEADS // SWA_KV_HEADS
SWA_WINDOW = 128
NA_HEADS = 16
NA_KH = 8
NA_KW = 16
POOL_WINDOWS = (2, 4, 8, 16)
POOL_DG = D // len(POOL_WINDOWS)
MLA_HEADS = 16
MLA_NOPE = 64
MLA_ROPE = 32
MLA_V = 64
MLA_QK = MLA_NOPE + MLA_ROPE
MLA_Q_RANK = 256
MLA_KV_RANK = 128

LANES = 128
SUBLANES = 8
MXU_WIDTH = 256
PROJ_CHUNK = MXU_WIDTH
VMEM_LIMIT = 56 * 1024 * 1024

NA_QROWS = 4
NA_WROWS = 12
NA_QB = NA_QROWS * GRID_W
NA_WB = NA_WROWS * GRID_W


def _cparams(n_axes):
    return pltpu.CompilerParams(dimension_semantics=("parallel",) * n_axes, vmem_limit_bytes=VMEM_LIMIT)


LOG2E = 1.4426950408889634


def _const_spec(shape):
    nd = len(shape)
    return pl.BlockSpec(shape, lambda *_: (0,) * nd, pipeline_mode=pl.Buffered(1))


def _dot(a, b):
    return jnp.dot(a, b, preferred_element_type=F32)


def _dot_nt(a, b):
    return lax.dot_general(a, b, (((1,), (1,)), ((), ())), preferred_element_type=F32)


def _silu(x):
    return x / (1.0 + jnp.exp(-x))


def _lane_iota():
    return lax.broadcasted_iota(jnp.int32, (1, LANES), 1)


def _modnorm(h, g, shift, scale):
    ms = jnp.mean(h * h, axis=-1, keepdims=True)
    y = h * lax.rsqrt(ms + EPS) * g
    return y * (1.0 + scale) + shift


def _rms_full(x, g, n):
    ms = jnp.sum(x * x, axis=-1, keepdims=True) * (1.0 / n)
    return x * lax.rsqrt(ms + EPS) * g


def _rms_head64(y, g):
    lo = _lane_iota() < HEAD_DIM
    x2 = y * y
    sa = jnp.sum(jnp.where(lo, x2, 0.0), axis=-1, keepdims=True)
    sb = jnp.sum(jnp.where(lo, 0.0, x2), axis=-1, keepdims=True)
    ms = jnp.where(lo, sa, sb) * (1.0 / HEAD_DIM)
    return y * lax.rsqrt(ms + EPS) * g


def _rms_head64_mxu(y, g, ones_bd):
    x2 = y * y
    hi = x2.astype(BF16)
    lo = (x2 - hi.astype(F32)).astype(BF16)
    ms = (_dot(hi, ones_bd) + _dot(lo, ones_bd)) * (1.0 / HEAD_DIM)
    return y * lax.rsqrt(ms + EPS) * g


def _rope(y, cos, sin, half):
    first = (_lane_iota() % (2 * half)) < half
    sw = jnp.where(first, pltpu.roll(y, LANES - half, 1), pltpu.roll(y, half, 1))
    return y * cos + sw * sin


def _dup_halves(y):
    lo = _lane_iota() < HEAD_DIM
    yr = pltpu.roll(y, HEAD_DIM, 1)
    return jnp.where(lo, y, yr), jnp.where(lo, yr, y)


def _exp2_parts(parts, sink=None):
    m = None
    for s in parts:
        mi = jnp.max(s, axis=-1, keepdims=True)
        m = mi if m is None else jnp.maximum(m, mi)
    if sink is not None:
        m = jnp.maximum(m, sink)
    es = [jnp.exp2(s - m).astype(BF16) for s in parts]
    return es, (None if sink is None else jnp.exp2(sink - m))


def _pv(es, vals):
    o = None
    for e, v in zip(es, vals):
        oi = _dot(e, v)
        o = oi if o is None else o + oi
    return o


def _norm_lo(o, extra=None):
    den = pltpu.roll(o, HEAD_DIM, 1)
    return o / (den if extra is None else den + extra)


def _norm_hi(o, extra=None):
    return pltpu.roll(o, HEAD_DIM, 1) / (o if extra is None else o + extra)


def _pipelined(n, scores_fn, finish_fn, ahead=1):
    pending = [scores_fn(i) for i in range(min(ahead, n))]
    for i in range(n):
        if i + ahead < n:
            pending.append(scores_fn(i + ahead))
        finish_fn(i, pending.pop(0))


def _with_ones(v):
    lo = _lane_iota() < HEAD_DIM
    vf = v.astype(F32)
    one = jnp.ones_like(vf)
    return (jnp.where(lo, vf, one).astype(BF16),
            jnp.where(lo, pltpu.roll(vf, HEAD_DIM, 1), one).astype(BF16))


def _ada_kernel(c_ref, w_ref, b_ref, o_ref):
    cond = _silu(c_ref[...]).astype(BF16)
    o_ref[0] = _dot(cond, w_ref[0].astype(BF16)) + b_ref[0]


def _ada(cvec, w_ada, b_ada):
    rows = cvec.shape[0]
    nblk = 6
    return pl.pallas_call(
        _ada_kernel,
        out_shape=jax.ShapeDtypeStruct((DEPTH, rows, 6 * D), F32),
        grid=(DEPTH, nblk),
        in_specs=[pl.BlockSpec((rows, D), lambda i, j: (0, 0)),
                  pl.BlockSpec((1, D, D), lambda i, j: (i, 0, j)),
                  pl.BlockSpec((1, 1, D), lambda i, j: (i, 0, j))],
        out_specs=pl.BlockSpec((1, rows, D), lambda i, j: (i, 0, j)),
        compiler_params=_cparams(2),
        name="ada",
    )(cvec, w_ada, b_ada.reshape(DEPTH, 1, 6 * D))


def _mod_spec(mod, batch_axis):
    if mod.shape[0] == 1:
        return pl.BlockSpec((1, 6, D), lambda *ids: (0, 0, 0))
    return pl.BlockSpec((1, 6, D), lambda *ids: (ids[batch_axis], 0, 0))


def _row_tile(L):
    return min(L, 512)


def _rope_table_specs(cos, sin, tm, rope):
    if rope:
        spec = pl.BlockSpec((tm, LANES), lambda b, i: (i, 0))
        return [spec, spec], [cos, sin]
    spec = pl.BlockSpec((tm, LANES), lambda b, i: (0, 0))
    return [spec, spec], [cos[:tm], sin[:tm]]


FFN_CHUNK = 256


def _tail_kernel(*refs, with_oproj):
    if with_oproj:
        o_ref, wo_ref, h_ref, mod_ref, g_ref, wgu_ref, wd_ref, out_ref, act_ref = refs
        h = h_ref[0] + mod_ref[0, 2:3, :] * _dot(o_ref[0], wo_ref[...])
    else:
        h_ref, mod_ref, g_ref, wgu_ref, wd_ref, out_ref, act_ref = refs
        h = h_ref[0]
    a = _modnorm(h, g_ref[...], mod_ref[0, 3:4, :], mod_ref[0, 4:5, :]).astype(BF16)
    for c in range(D_FF // FFN_CHUNK):
        lo = c * FFN_CHUNK
        gate = _dot(a, wgu_ref[:, lo:lo + FFN_CHUNK])
        up = _dot(a, wgu_ref[:, D_FF + lo:D_FF + lo + FFN_CHUNK])
        act_ref[:, lo:lo + FFN_CHUNK] = (_silu(gate) * up).astype(BF16)
    y = _dot(act_ref[...], wd_ref[...])
    out_ref[0] = h + mod_ref[0, 5:6, :] * y


def _tail(o, wo, h, mod, g, wgu, wd):
    B, L, _ = h.shape
    tm = _row_tile(L)
    row = pl.BlockSpec((1, tm, D), lambda b, i: (b, i, 0))
    with_oproj = o is not None
    specs = [row, _mod_spec(mod, 0), _const_spec((1, D)), _const_spec((D, 2 * D_FF)), _const_spec((D_FF, D))]
    args = [h, mod, g.reshape(1, D), wgu, wd]
    if with_oproj:
        specs = [row, _const_spec((D, D))] + specs
        args = [o, wo] + args
    return pl.pallas_call(
        functools.partial(_tail_kernel, with_oproj=with_oproj),
        out_shape=jax.ShapeDtypeStruct(h.shape, F32),
        grid=(B, L // tm),
        in_specs=specs,
        out_specs=row,
        scratch_shapes=[pltpu.VMEM((tm, D_FF), BF16)],
        compiler_params=_cparams(2),
        name="tail" if with_oproj else "ffn",
    )(*args)


def _axial_tables(L, half, lead, width):
    t = np.arange(L)
    freqs = ROPE_THETA ** (-np.arange(half, dtype=np.float64) / half)
    cos = np.ones((L, LANES), np.float64)
    sin = np.zeros((L, LANES), np.float64)
    for base in range(0, LANES, width):
        for axis, pos in enumerate((t // GRID_W, t % GRID_W)):
            ang = pos[:, None].astype(np.float64) * freqs[None, :]
            o = base + lead + axis * 2 * half
            cos[:, o:o + half] = np.cos(ang)
            cos[:, o + half:o + 2 * half] = np.cos(ang)
            sin[:, o:o + half] = -np.sin(ang)
            sin[:, o + half:o + 2 * half] = np.sin(ang)
    return jnp.asarray(cos, F32), jnp.asarray(sin, F32)


SWA_NQ = SWA_HEADS * HEAD_DIM
SWA_NK = SWA_KV_HEADS * HEAD_DIM
SWA_SCALE = HEAD_DIM ** -0.5 * LOG2E


def _swa_proj_kernel(h_ref, mod_ref, g_ref, w_ref, gq_ref, gk_ref, ones_ref, cos_ref, sin_ref,
                     q_ref, k_ref, v_ref, *, rope):
    a = _modnorm(h_ref[0], g_ref[...], mod_ref[0, 0:1, :], mod_ref[0, 1:2, :]).astype(BF16)
    ones_bd = ones_ref[...]
    lo = _lane_iota() < HEAD_DIM
    n_q = SWA_NQ // PROJ_CHUNK

    def normed(blk, gain):
        blk = _rms_head64_mxu(blk, gain, ones_bd)
        return _rope(blk, cos_ref[...], sin_ref[...], HEAD_DIM // 4) if rope else blk

    def matmul(c):
        return _dot(a, w_ref[:, c * PROJ_CHUNK:(c + 1) * PROJ_CHUNK])

    def epilogue(c, y):
        for half in range(PROJ_CHUNK // LANES):
            blk = y[:, half * LANES:(half + 1) * LANES]
            if c < n_q:
                cb = c * (PROJ_CHUNK // LANES) + half
                q_ref[0, :, cb * LANES:(cb + 1) * LANES] = normed(blk, gq_ref[...]).astype(BF16)
                continue
            out_ref = k_ref if c == n_q else v_ref
            if c == n_q:
                first, second = _dup_halves(normed(blk, gk_ref[...]))
            else:
                first, second = jnp.where(lo, blk, 1.0), jnp.where(lo, pltpu.roll(blk, HEAD_DIM, 1), 1.0)
            out_ref[0, :, (2 * half) * LANES:(2 * half + 1) * LANES] = first.astype(BF16)
            out_ref[0, :, (2 * half + 1) * LANES:(2 * half + 2) * LANES] = second.astype(BF16)

    _pipelined(n_q + 2, matmul, epilogue)


def _head_ones():
    half = np.arange(LANES) // HEAD_DIM
    return jnp.asarray(half[:, None] == half[None, :], BF16)


def _swa_proj(h, mod, g, w, gq, gk, cos, sin, rope):
    B, L, _ = h.shape
    tm = _row_tile(L)
    nkv = 2 * SWA_NK
    row = lambda b, i: (b, i, 0)
    table_specs, tables = _rope_table_specs(cos, sin, tm, rope)
    return pl.pallas_call(
        functools.partial(_swa_proj_kernel, rope=rope),
        out_shape=(jax.ShapeDtypeStruct((B, L, SWA_NQ), BF16),
                   jax.ShapeDtypeStruct((B, L, nkv), BF16),
                   jax.ShapeDtypeStruct((B, L, nkv), BF16)),
        grid=(B, L // tm),
        in_specs=[pl.BlockSpec((1, tm, D), row),
                  _mod_spec(mod, 0),
                  _const_spec((1, D)),
                  _const_spec((D, SWA_NQ + 2 * SWA_NK)),
                  _const_spec((1, LANES)),
                  _const_spec((1, LANES)),
                  _const_spec((LANES, LANES))] + table_specs,
        out_specs=(pl.BlockSpec((1, tm, SWA_NQ), row),
                   pl.BlockSpec((1, tm, nkv), row),
                   pl.BlockSpec((1, tm, nkv), row)),
        compiler_params=_cparams(2),
        name="swa_proj",
    )(h, mod, g.reshape(1, D), w, (jnp.tile(gq, 2) * SWA_SCALE).reshape(1, LANES),
      jnp.tile(gk, 2).reshape(1, LANES), _head_ones(), *tables)


SWA_QB = 256
SWA_SPAN = SWA_QB + 2 * SWA_WINDOW
SWA_STEP_BLOCKS = 4


def _swa_band_masks():
    iq = np.arange(SWA_QB)[:, None]
    ik = np.arange(SWA_SPAN)[None, :]
    masks = [np.where(np.abs(o * SWA_WINDOW + iq - ik) <= SWA_WINDOW, 0.0, NEG_INF) for o in range(3)]
    return jnp.asarray(np.stack(masks), F32)


def _swa_attn_kernel(*refs, with_window, L, nblocks):
    if with_window:
        q_ref, kd_ref, vd_ref, kc_ref, vc_ref, sink_ref, band_ref, o_ref = refs
    else:
        q_ref, kc_ref, vc_ref, sink_ref, o_ref = refs
    lo = _lane_iota() < HEAD_DIM

    def window_start(bi):
        start = (pl.program_id(1) * nblocks + bi) * SWA_QB
        w0 = pl.multiple_of(jnp.clip(start - SWA_WINDOW, 0, L - SWA_SPAN), SWA_WINDOW)
        return start, w0

    def keys_or_values(win_ref, ctx_ref, bi, cols):
        if not with_window:
            return ctx_ref[0, :, cols]
        _, w0 = window_start(bi)
        return jnp.concatenate([win_ref[0, pl.ds(w0, SWA_SPAN), cols], ctx_ref[0, :, cols]], axis=0)

    def scores(c):
        bi, hk = divmod(c, SWA_KV_HEADS)
        qrows = slice(bi * SWA_QB, (bi + 1) * SWA_QB)
        qs = []
        for g in range(SWA_GROUPS):
            cb = hk * 2 + g // 2
            qc = q_ref[0, qrows, cb * LANES:(cb + 1) * LANES]
            zero = jnp.zeros_like(qc)
            qs.append(jnp.where(lo, qc, zero) if g % 2 == 0 else jnp.where(lo, zero, qc))
        qstack = jnp.concatenate(qs, axis=0)
        cols = slice(hk * LANES, (hk + 1) * LANES)
        s = _dot_nt(qstack, keys_or_values(kd_ref if with_window else None, kc_ref, bi, cols))
        if with_window:
            start, w0 = window_start(bi)
            band = band_ref[(start - w0) // SWA_WINDOW]
            band = jnp.concatenate([band] * SWA_GROUPS, axis=0)
            s = jnp.concatenate([s[:, :SWA_SPAN] + band, s[:, SWA_SPAN:]], axis=1)
        return [s]

    def finish(c, parts):
        bi, hk = divmod(c, SWA_KV_HEADS)
        qrows = slice(bi * SWA_QB, (bi + 1) * SWA_QB)
        cols = slice(hk * LANES, (hk + 1) * LANES)
        vals = [keys_or_values(vd_ref if with_window else None, vc_ref, bi, cols)]
        es, sink_e = [], []
        for g in range(SWA_GROUPS):
            head_rows = slice(g * SWA_QB, (g + 1) * SWA_QB)
            eg, sg = _exp2_parts([s[head_rows] for s in parts], sink_ref[hk * SWA_GROUPS + g] * LOG2E)
            es.append(eg[0])
            sink_e.append(sg)
        o = _pv([jnp.concatenate(es, axis=0)], vals)
        for j in range(2):
            even = slice((2 * j) * SWA_QB, (2 * j + 1) * SWA_QB)
            odd = slice((2 * j + 1) * SWA_QB, (2 * j + 2) * SWA_QB)
            pair = jnp.where(lo, _norm_lo(o[even], sink_e[2 * j]), _norm_hi(o[odd], sink_e[2 * j + 1]))
            o_ref[0, qrows, (hk * 2 + j) * LANES:(hk * 2 + j + 1) * LANES] = pair.astype(BF16)

    n_chains = nblocks * SWA_KV_HEADS
    _pipelined(n_chains, scores, finish, ahead=1 if with_window else n_chains)


def _swa_attn(q, kd, vd, kcd, vcd, sink):
    B, L, _ = q.shape
    Lc = kcd.shape[1]
    nkv = 2 * SWA_NK
    with_window = kd is not None
    nblocks = min(SWA_STEP_BLOCKS, L // SWA_QB)
    blk = lambda b, i: (b, i, 0)
    whole = lambda b, i: (b, 0, 0)
    specs = [pl.BlockSpec((1, nblocks * SWA_QB, SWA_NQ), blk)]
    args = [q]
    if with_window:
        specs += [pl.BlockSpec((1, L, nkv), whole), pl.BlockSpec((1, L, nkv), whole)]
        args += [kd, vd]
    specs += [pl.BlockSpec((1, Lc, nkv), whole), pl.BlockSpec((1, Lc, nkv), whole),
              pl.BlockSpec(memory_space=pltpu.SMEM)]
    args += [kcd, vcd, sink]
    if with_window:
        specs.append(_const_spec((3, SWA_QB, SWA_SPAN)))
        args.append(_swa_band_masks())
    return pl.pallas_call(
        functools.partial(_swa_attn_kernel, with_window=with_window, L=L, nblocks=nblocks),
        out_shape=jax.ShapeDtypeStruct((B, L, SWA_NQ), BF16),
        grid=(B, L // (nblocks * SWA_QB)),
        in_specs=specs,
        out_specs=pl.BlockSpec((1, nblocks * SWA_QB, SWA_NQ), blk),
        compiler_params=_cparams(2),
        name="swa_attn" if with_window else "swa_attn_ctx",
    )(*args)


NA_N = NA_HEADS * HEAD_DIM
NA_SCALE = HEAD_DIM ** -0.5 * LOG2E


def _na_proj_kernel(h_ref, mod_ref, g_ref, w_ref, gq_ref, gk_ref, q_ref, k_ref, v_ref):
    a = _modnorm(h_ref[0], g_ref[...], mod_ref[0, 0:1, :], mod_ref[0, 1:2, :]).astype(BF16)
    per_kind = NA_N // PROJ_CHUNK

    def matmul(c):
        return _dot(a, w_ref[:, c * PROJ_CHUNK:(c + 1) * PROJ_CHUNK])

    def epilogue(c, y):
        kind, cc = divmod(c, per_kind)
        for half in range(PROJ_CHUNK // LANES):
            blk = y[:, half * LANES:(half + 1) * LANES]
            cols = slice(cc * PROJ_CHUNK + half * LANES, cc * PROJ_CHUNK + (half + 1) * LANES)
            if kind == 0:
                q_ref[0, :, cols] = _rms_head64(blk, gq_ref[...]).astype(BF16)
            elif kind == 1:
                k_ref[0, :, cols] = _rms_head64(blk, gk_ref[...]).astype(BF16)
            else:
                v_ref[0, :, cols] = blk.astype(BF16)

    _pipelined(3 * per_kind, matmul, epilogue)


def _na_proj(h, mod, g, w, gq, gk):
    B, L, _ = h.shape
    tm = _row_tile(L)
    row = lambda b, i: (b, i, 0)
    out = jax.ShapeDtypeStruct((B, L, NA_N), BF16)
    return pl.pallas_call(
        _na_proj_kernel,
        out_shape=(out, out, out),
        grid=(B, L // tm),
        in_specs=[pl.BlockSpec((1, tm, D), row),
                  _mod_spec(mod, 0),
                  _const_spec((1, D)),
                  _const_spec((D, 3 * NA_N)),
                  _const_spec((1, LANES)),
                  _const_spec((1, LANES))],
        out_specs=(pl.BlockSpec((1, tm, NA_N), row),) * 3,
        compiler_params=_cparams(2),
        name="na_proj",
    )(h, mod, g.reshape(1, D), w, (jnp.tile(gq, 2) * NA_SCALE).reshape(1, LANES),
      jnp.tile(gk, 2).reshape(1, LANES))


NA_PAIRS = NA_WROWS // 2
NA_OFFSETS = 2 * NA_KH
NA_MASKED_KIND = (1, 0)


def _na_bias_tiles(rpb, kinds):
    H = rpb.shape[0]
    c = np.arange(GRID_W)
    col_start = np.clip(c - NA_KW // 2, 0, GRID_W - NA_KW)
    col_ok = (c[None, :] >= col_start[:, None]) & (c[None, :] < col_start[:, None] + NA_KW)
    dc = np.clip(c[None, :] - c[:, None], -(NA_KW - 1), NA_KW - 1) + NA_KW - 1
    onehot = (dc[None] == np.arange(2 * NA_KW - 1)[:, None, None]).astype(np.float32)
    e = jnp.einsum("hrd,dqk->hrqk", rpb.astype(F32) * LOG2E, jnp.asarray(onehot),
                   precision=lax.Precision.HIGHEST)
    e = jnp.where(jnp.asarray(col_ok), e, NEG_INF)
    masked = jnp.full((H, 1, GRID_W, GRID_W), NEG_INF, F32)
    ext = jnp.concatenate([masked, e, masked], axis=1)
    off = masked[:, 0]
    tiles = []
    for variant, shifted in kinds:
        first = ext[:, shifted] if variant != 2 else off
        second = ext[:, shifted + 1] if variant != 1 else off
        tiles.append(jnp.concatenate([first, second], axis=-1))
    return jnp.stack(tiles, axis=1)


def _na_tile_ids(rows):
    kh = min(NA_KH, rows)
    nblk = rows // NA_QROWS
    ids = np.zeros((nblk, NA_QROWS, NA_PAIRS), np.int32)
    kinds = []
    starts = []
    for g in range(nblk):
        ws = int(np.clip(NA_QROWS * g - NA_QROWS, 0, rows - NA_WROWS))
        starts.append(ws)
        for i in range(NA_QROWS):
            r = NA_QROWS * g + i
            r0 = int(np.clip(r - kh // 2, 0, rows - kh))
            for kp in range(NA_PAIRS):
                rk = ws + 2 * kp
                ok1 = r0 <= rk < r0 + kh
                ok2 = r0 <= rk + 1 < r0 + kh
                shifted = rk - r + NA_KH - 1 + 1
                if ok1 and ok2:
                    variant = 0
                elif ok1:
                    variant = 1
                elif ok2:
                    variant = 2
                else:
                    variant, shifted = NA_MASKED_KIND
                assert 0 <= shifted < NA_OFFSETS
                if (variant, shifted) not in kinds:
                    kinds.append((variant, shifted))
                ids[g, i, kp] = kinds.index((variant, shifted))
    return ids, kinds, starts


def _na_attn_kernel(*refs, with_lat, npairs, L):
    lo = _lane_iota() < HEAD_DIM
    if with_lat:
        q_ref, k_ref, v_ref, kc_ref, vc_ref, tile_ref, o_ref, v1_ref = refs
        ids, kinds, starts = _na_tile_ids(L // GRID_W)
        tq = NA_QB
    else:
        q_ref, kc_ref, vc_ref, o_ref = refs
        tq = L
    nblk = L // tq
    if with_lat:
        v1_ref[0], v1_ref[1] = _with_ones(v_ref[0])
    vc_ones = [_with_ones(vc_ref[0, :, p * LANES:(p + 1) * LANES]) for p in range(npairs)]

    def window(g):
        return slice(starts[g] * GRID_W, starts[g] * GRID_W + NA_WB)

    def scores(c):
        p, g = divmod(c, nblk)
        cols = slice(p * LANES, (p + 1) * LANES)
        q = q_ref[0, g * tq:(g + 1) * tq, cols]
        zero = jnp.zeros_like(q)
        qstack = jnp.concatenate([jnp.where(lo, q, zero), jnp.where(lo, zero, q)], axis=0)
        if not with_lat:
            return [_dot_nt(qstack, kc_ref[0, :, cols])]
        return [_dot_nt(qstack, jnp.concatenate([k_ref[0, window(g), cols], kc_ref[0, :, cols]], axis=0))]

    def window_softmax(g, s):
        rows = []
        for hh in range(2):
            for i in range(NA_QROWS):
                r0 = (hh * NA_QROWS + i) * GRID_W
                ctx = s[r0:r0 + GRID_W, NA_WB:]
                tiles = []
                for kp in range(NA_PAIRS):
                    slot = int(ids[g, i, kp])
                    if kinds[slot] == NA_MASKED_KIND:
                        tiles.append(None)
                    else:
                        tiles.append(s[r0:r0 + GRID_W, kp * LANES:(kp + 1) * LANES] + tile_ref[hh, slot])
                live = [t for t in tiles if t is not None] + [ctx[:, j * LANES:(j + 1) * LANES]
                                                              for j in range(ctx.shape[1] // LANES)]
                top = live[0]
                for t in live[1:]:
                    top = jnp.maximum(top, t)
                m = jnp.max(top, axis=-1, keepdims=True)
                zeros = jnp.zeros((GRID_W, LANES), BF16)
                nums = [zeros if t is None else jnp.exp2(t - m).astype(BF16) for t in tiles]
                rows.append(jnp.concatenate(nums + [jnp.exp2(ctx - m).astype(BF16)], axis=1))
        return jnp.concatenate(rows, axis=0)

    def finish(c, parts):
        p, g = divmod(c, nblk)
        es = [window_softmax(g, parts[0])] if with_lat else _exp2_parts(parts)[0]
        outs = []
        for hh in range(2):
            vals = vc_ones[p][hh]
            if with_lat:
                vals = jnp.concatenate([v1_ref[hh, window(g), :], vals], axis=0)
            outs.append(_pv([e[hh * tq:(hh + 1) * tq] for e in es], [vals]))
        pair = jnp.where(lo, _norm_lo(outs[0]), _norm_hi(outs[1]))
        o_ref[0, g * tq:(g + 1) * tq, p * LANES:(p + 1) * LANES] = pair.astype(BF16)

    _pipelined(npairs * nblk, scores, finish, ahead=1 if with_lat else npairs * nblk)


def _na_attn(q, k, v, kc, vc, tiles):
    B, L, _ = q.shape
    Lc = kc.shape[1]
    with_lat = k is not None
    scratch = []
    if with_lat:
        npairs = 1
        grid = (NA_HEADS // 2, B)
        blk = lambda n: pl.BlockSpec((1, n, LANES), lambda p, b: (b, 0, p))
        specs = [blk(L), blk(L), blk(L), blk(Lc), blk(Lc),
                 pl.BlockSpec((2,) + tiles.shape[1:], lambda p, b: (p, 0, 0, 0))]
        args = [q, k, v, kc, vc, tiles]
        scratch = [pltpu.VMEM((2, L, LANES), BF16)]
        out_spec = blk(L)
    else:
        npairs = NA_HEADS // 2
        grid = (B,)
        blk = lambda n: pl.BlockSpec((1, n, NA_N), lambda b: (b, 0, 0))
        specs = [blk(L), blk(Lc), blk(Lc)]
        args = [q, kc, vc]
        out_spec = blk(L)
    return pl.pallas_call(
        functools.partial(_na_attn_kernel, with_lat=with_lat, npairs=npairs, L=L),
        out_shape=jax.ShapeDtypeStruct((B, L, NA_N), BF16),
        grid=grid,
        in_specs=specs,
        out_specs=out_spec,
        scratch_shapes=scratch,
        compiler_params=_cparams(len(grid)),
        name="na_attn" if with_lat else "na_attn_ctx",
    )(*args)


POOL_HALO = SUBLANES


def _pool_bands(tm):
    t = np.arange(tm)[:, None]
    j = np.arange(tm + 2 * POOL_HALO)[None, :] - POOL_HALO
    return jnp.asarray(np.stack([(j >= t - w // 2) & (j < t - w // 2 + w) for w in POOL_WINDOWS]), BF16)


def _pool_kernel(h_ref, hp_ref, hn_ref, mod_ref, g_ref, band_ref, w_ref, b_ref, sc_ref, o_ref, *, tm, L):
    i = pl.program_id(1)
    nt = L // tm
    g = g_ref[...]
    shift, scale = mod_ref[0, 0:1, :], mod_ref[0, 1:2, :]
    h = h_ref[0]
    a = _modnorm(h, g, shift, scale)
    prev_on = (i > 0).astype(F32)
    next_on = (i < nt - 1).astype(F32)
    ext = jnp.concatenate([_modnorm(hp_ref[0], g, shift, scale) * prev_on, a,
                           _modnorm(hn_ref[0], g, shift, scale) * next_on], axis=0)
    ext_hi = ext.astype(BF16)
    ext_lo = (ext - ext_hi.astype(F32)).astype(BF16)
    t = i * tm + lax.broadcasted_iota(jnp.int32, (tm, 1), 0)
    group_cols = [slice(gi * POOL_DG, (gi + 1) * POOL_DG) for gi in range(len(POOL_WINDOWS))]
    sums = [_dot(band_ref[gi], ext_hi[:, cols]) + _dot(band_ref[gi], ext_lo[:, cols])
            for gi, cols in enumerate(group_cols)]
    pooled = []
    for gi, w in enumerate(POOL_WINDOWS):
        first = jnp.clip(t - w // 2, 0, L)
        last = jnp.clip(t - w // 2 + w, 0, L)
        cnt = (last - first).astype(F32)
        pooled.append((sums[gi] / cnt - a[:, group_cols[gi]]).astype(BF16))
    ys = [_dot(pooled[gi], w_ref[gi]) + b_ref[gi:gi + 1, :] for gi in range(len(POOL_WINDOWS))]
    y = jnp.concatenate(ys, axis=-1) * sc_ref[...]
    o_ref[0] = h + mod_ref[0, 2:3, :] * y


def _pool(h, mod, g, w, b, sc):
    B, L, _ = h.shape
    tm = min(L, 256)
    hb = tm // POOL_HALO
    last_halo = L // POOL_HALO - 1
    return pl.pallas_call(
        functools.partial(_pool_kernel, tm=tm, L=L),
        out_shape=jax.ShapeDtypeStruct(h.shape, F32),
        grid=(B, L // tm),
        in_specs=[pl.BlockSpec((1, tm, D), lambda b_, i: (b_, i, 0)),
                  pl.BlockSpec((1, POOL_HALO, D), lambda b_, i: (b_, jnp.maximum(i * hb - 1, 0), 0)),
                  pl.BlockSpec((1, POOL_HALO, D), lambda b_, i: (b_, jnp.minimum((i + 1) * hb, last_halo), 0)),
                  _mod_spec(mod, 0),
                  _const_spec((1, D)),
                  _const_spec((len(POOL_WINDOWS), tm, tm + 2 * POOL_HALO)),
                  _const_spec((len(POOL_WINDOWS), POOL_DG, POOL_DG)),
                  _const_spec((len(POOL_WINDOWS), POOL_DG)),
                  _const_spec((1, D))],
        out_specs=pl.BlockSpec((1, tm, D), lambda b_, i: (b_, i, 0)),
        compiler_params=_cparams(2),
        name="pool",
    )(h, h, h, mod, g.reshape(1, D), _pool_bands(tm), w, b, sc.reshape(1, D))


MLA_A_COLS = 512
MLA_HQ = MLA_HEADS * LANES
MLA_SCALE = MLA_QK ** -0.5 * LOG2E
MLA_X2_LANE = LANES - MLA_ROPE // 2
MLA_X1_LANE = MLA_X2_LANE - LANES // 2


def _mla_proj_kernel(*refs, rope, need_q):
    (h_ref, mod_ref, g_ref, wa_ref, gcq_ref, gckv_ref, wuq_ref, wuk_ref, wuv_ref,
     gq_ref, gk_ref, ones_ref, cos_ref, sin_ref) = refs[:14]
    outs = refs[14:]
    if need_q:
        q_ref, k_ref, v_ref = outs
    else:
        k_ref, v_ref = outs
    a = _modnorm(h_ref[0], g_ref[...], mod_ref[0, 0:1, :], mod_ref[0, 1:2, :]).astype(BF16)
    proj = _dot(a, wa_ref[...])
    def rotary(blk):
        return blk * cos_ref[...] + pltpu.roll(blk, LANES // 2, 1) * sin_ref[...] if rope else blk

    if need_q:
        cq = _rms_full(proj[:, :MLA_Q_RANK], gcq_ref[...], MLA_Q_RANK).astype(BF16)
        ones = ones_ref[...]
    ckv = _rms_full(proj[:, MLA_Q_RANK:MLA_Q_RANK + MLA_KV_RANK], gckv_ref[...], MLA_KV_RANK).astype(BF16)
    kr = proj[:, MLA_Q_RANK + MLA_KV_RANK:]
    kr_rot = rotary(kr * gk_ref[...])
    kr_ss = jnp.sum(kr * kr, axis=-1, keepdims=True)

    def q_head(blk):
        x2 = blk * blk
        hi = x2.astype(BF16)
        lo = (x2 - hi.astype(F32)).astype(BF16)
        ms = (_dot(hi, ones) + _dot(lo, ones)) * (1.0 / MLA_QK)
        return rotary(blk * lax.rsqrt(ms + EPS) * gq_ref[...])

    def k_head(blk):
        ms = (jnp.sum(blk * blk, axis=-1, keepdims=True) + kr_ss) * (1.0 / MLA_QK)
        return (blk * gk_ref[...] + kr_rot) * lax.rsqrt(ms + EPS)

    heads_per_chunk = PROJ_CHUNK // LANES
    chunks = []
    for j in range(MLA_HEADS // heads_per_chunk):
        chunks += ([("q", j)] if need_q else []) + [("k", j)]
    chunks += [("v", j) for j in range(MLA_HEADS * MLA_V // PROJ_CHUNK)]

    def matmul(c):
        kind, j = chunks[c]
        cols = slice(j * PROJ_CHUNK, (j + 1) * PROJ_CHUNK)
        if kind == "q":
            return _dot(cq, wuq_ref[:, cols])
        return _dot(ckv, wuk_ref[:, cols] if kind == "k" else wuv_ref[:, cols])

    def epilogue(c, y):
        kind, j = chunks[c]
        if kind == "v":
            v_ref[0, :, j * PROJ_CHUNK:(j + 1) * PROJ_CHUNK] = y.astype(BF16)
            return
        for half in range(heads_per_chunk):
            blk = y[:, half * LANES:(half + 1) * LANES]
            cols = slice((j * heads_per_chunk + half) * LANES, (j * heads_per_chunk + half + 1) * LANES)
            if kind == "q":
                q_ref[0, :, cols] = q_head(blk).astype(BF16)
            else:
                k_ref[0, :, cols] = k_head(blk).astype(BF16)

    _pipelined(len(chunks), matmul, epilogue)


def _mla_proj(h, mod, g, wts, cos, sin, rope, need_q):
    B, L, _ = h.shape
    tm = _row_tile(L)
    row = lambda b, i: (b, i, 0)
    out_shape = [jax.ShapeDtypeStruct((B, L, MLA_HQ), BF16), jax.ShapeDtypeStruct((B, L, MLA_HEADS * MLA_V), BF16)]
    out_specs = [pl.BlockSpec((1, tm, MLA_HQ), row), pl.BlockSpec((1, tm, MLA_HEADS * MLA_V), row)]
    if need_q:
        out_shape.insert(0, jax.ShapeDtypeStruct((B, L, MLA_HQ), BF16))
        out_specs.insert(0, pl.BlockSpec((1, tm, MLA_HQ), row))
    table_specs, tables = _rope_table_specs(cos, sin, tm, rope)
    return pl.pallas_call(
        functools.partial(_mla_proj_kernel, rope=rope, need_q=need_q),
        out_shape=tuple(out_shape),
        grid=(B, L // tm),
        in_specs=[pl.BlockSpec((1, tm, D), row),
                  _mod_spec(mod, 0),
                  _const_spec((1, D)),
                  _const_spec((D, MLA_A_COLS)),
                  _const_spec((1, MLA_Q_RANK)),
                  _const_spec((1, MLA_KV_RANK)),
                  _const_spec((MLA_Q_RANK, MLA_HQ)),
                  _const_spec((MLA_KV_RANK, MLA_HQ)),
                  _const_spec((MLA_KV_RANK, MLA_HEADS * MLA_V)),
                  _const_spec((1, LANES)),
                  _const_spec((1, LANES)),
                  _const_spec((LANES, LANES))] + table_specs,
        out_specs=tuple(out_specs),
        compiler_params=_cparams(2),
        name="mla_proj" if need_q else "mla_proj_kv",
    )(h, mod, g.reshape(1, D), wts["wa"], wts["gcq"], wts["gckv"], wts["wuq"], wts["wuk"], wts["wuv"],
      wts["gq"], wts["gk"], jnp.ones((LANES, LANES), BF16), *tables)


def _mla_head_lanes(nope, rope):
    q = MLA_ROPE // 4
    ref = nope if nope is not None else rope
    zeros = lambda n: jnp.zeros(ref.shape[:-1] + (n,), ref.dtype)
    n_a = MLA_X1_LANE
    n_b = MLA_NOPE - n_a
    pad = MLA_X2_LANE - (MLA_X1_LANE + 2 * q + n_b)
    nope_a, nope_b = (zeros(n_a), zeros(n_b)) if nope is None else (nope[..., :n_a], nope[..., n_a:])
    if rope is None:
        x1, x2 = zeros(2 * q), zeros(2 * q)
    else:
        x1 = jnp.concatenate([rope[..., 0:q], rope[..., 2 * q:3 * q]], axis=-1)
        x2 = jnp.concatenate([rope[..., q:2 * q], rope[..., 3 * q:4 * q]], axis=-1)
    return jnp.concatenate([nope_a, x1, nope_b, zeros(pad), x2], axis=-1)


def _mla_rope_tables(L):
    quarter = MLA_ROPE // 4
    t = np.arange(L)
    freqs = ROPE_THETA ** (-np.arange(quarter, dtype=np.float64) / quarter)
    cos = np.ones((L, LANES), np.float64)
    sin = np.zeros((L, LANES), np.float64)
    for axis, pos in enumerate((t // GRID_W, t % GRID_W)):
        ang = pos[:, None].astype(np.float64) * freqs[None, :]
        x1 = slice(MLA_X1_LANE + axis * quarter, MLA_X1_LANE + (axis + 1) * quarter)
        x2 = slice(MLA_X2_LANE + axis * quarter, MLA_X2_LANE + (axis + 1) * quarter)
        cos[:, x1] = np.cos(ang)
        cos[:, x2] = np.cos(ang)
        sin[:, x1] = -np.sin(ang)
        sin[:, x2] = np.sin(ang)
    return jnp.asarray(cos, F32), jnp.asarray(sin, F32)


def _mla_weights(w_a, g_cq, g_ckv, w_uq, w_ukv, g_q, g_k):
    H = MLA_HEADS
    spread = lambda t: _mla_head_lanes(t[..., :MLA_NOPE], t[..., MLA_NOPE:])
    n_lin = MLA_Q_RANK + MLA_KV_RANK
    wa = jnp.concatenate([w_a[:, :n_lin], _mla_head_lanes(None, w_a[:, n_lin:])], axis=1)
    wukv = w_ukv.reshape(MLA_KV_RANK, H, MLA_NOPE + MLA_V)
    wuk = _mla_head_lanes(wukv[:, :, :MLA_NOPE], None)
    return dict(wa=wa.astype(BF16),
                gcq=g_cq.reshape(1, MLA_Q_RANK), gckv=g_ckv.reshape(1, MLA_KV_RANK),
                wuq=spread(w_uq.reshape(MLA_Q_RANK, H, MLA_QK)).reshape(MLA_Q_RANK, MLA_HQ).astype(BF16),
                wuk=wuk.reshape(MLA_KV_RANK, MLA_HQ).astype(BF16),
                wuv=wukv[:, :, MLA_NOPE:].reshape(MLA_KV_RANK, H * MLA_V).astype(BF16),
                gq=(spread(g_q) * MLA_SCALE).reshape(1, LANES), gk=spread(g_k).reshape(1, LANES))


MLA_TQ = 512
MLA_KEY_CHUNK = 3 * MXU_WIDTH


def _mla_attn_kernel(q_ref, k_ref, kc_ref, v_ref, vc_ref, o_ref, kall_ref, v1_ref):
    lo = _lane_iota() < MLA_V
    L, Lc = k_ref.shape[1], kc_ref.shape[1]
    kall_ref[0:L, :] = k_ref[0]
    kall_ref[L:L + Lc, :] = kc_ref[0]
    v1_ref[0, 0:L, :], v1_ref[1, 0:L, :] = _with_ones(v_ref[0])
    v1_ref[0, L:L + Lc, :], v1_ref[1, L:L + Lc, :] = _with_ones(vc_ref[0])
    first_head = []

    def scores(c):
        qi, j = divmod(c, 2)
        cols = slice(j * LANES, (j + 1) * LANES)
        return [_dot_nt(q_ref[0, qi * MLA_TQ:(qi + 1) * MLA_TQ, cols], kall_ref[:, cols])]

    def finish(c, parts):
        qi, j = divmod(c, 2)
        s = parts[0]
        m = jnp.max(s, axis=-1, keepdims=True)
        o = None
        for k0 in range(0, s.shape[1], MLA_KEY_CHUNK):
            e = jnp.exp2(s[:, k0:k0 + MLA_KEY_CHUNK] - m).astype(BF16)
            part = _dot(e, v1_ref[j, k0:k0 + MLA_KEY_CHUNK, :])
            o = part if o is None else o + part
        if j == 0:
            first_head.append(_norm_lo(o))
        else:
            pair = jnp.where(lo, first_head.pop(), _norm_hi(o))
            o_ref[0, qi * MLA_TQ:(qi + 1) * MLA_TQ, :] = pair.astype(BF16)

    _pipelined(2 * (q_ref.shape[1] // MLA_TQ), scores, finish)


def _mla_attn(q, k, kc, v, vc):
    B, L, _ = q.shape
    Lc = kc.shape[1]
    npair = MLA_HEADS // 2
    return pl.pallas_call(
        _mla_attn_kernel,
        out_shape=jax.ShapeDtypeStruct((B, L, MLA_HEADS * MLA_V), BF16),
        grid=(B, npair),
        in_specs=[pl.BlockSpec((1, L, 2 * LANES), lambda b, p: (b, 0, p)),
                  pl.BlockSpec((1, L, 2 * LANES), lambda b, p: (b, 0, p)),
                  pl.BlockSpec((1, Lc, 2 * LANES), lambda b, p: (b, 0, p)),
                  pl.BlockSpec((1, L, LANES), lambda b, p: (b, 0, p)),
                  pl.BlockSpec((1, Lc, LANES), lambda b, p: (b, 0, p))],
        out_specs=pl.BlockSpec((1, L, LANES), lambda b, p: (b, 0, p)),
        scratch_shapes=[pltpu.VMEM((L + Lc, 2 * LANES), BF16), pltpu.VMEM((2, L + Lc, LANES), BF16)],
        compiler_params=_cparams(2),
        name="mla_attn",
    )(q, k, kc, v, vc)


def kernel(x, c, ctx, c_ctx, w_ada, b_ada, g_mix, g_ffn, w_gate_up, w_down,
           swa_w_qkv, swa_g_q, swa_g_k, swa_sink, swa_w_o,
           na_w_qkv, na_g_q, na_g_k, na_rpb, na_w_o,
           pool_w, pool_b, pool_scale,
           mla_w_a, mla_g_cq, mla_g_ckv, mla_w_uq, mla_w_ukv, mla_g_q, mla_g_k, mla_w_o):
    B, L, _ = x.shape
    n_rows = -(-(B + 1) // SUBLANES) * SUBLANES
    cvec = jnp.concatenate([c, c_ctx[None, :], jnp.zeros((n_rows - B - 1, D), F32)], axis=0)
    mods = _ada(cvec, w_ada, b_ada).reshape(DEPTH, n_rows, 6, D)

    cos_a, sin_a = _axial_tables(L, HEAD_DIM // 4, 0, HEAD_DIM)
    cos_d, sin_d = _mla_rope_tables(L)

    flat = lambda t: t.reshape(1, -1, t.shape[-1])
    per_batch = lambda t: t.reshape(B, -1, t.shape[-1])

    h, hc = x, ctx
    for i in range(DEPTH):
        m, j = i % 4, i // 4
        need_ctx = i < DEPTH - 1
        mod = mods[i, :B]
        modc = mods[i, B:B + 1]
        o = oc = wo = None
        if m == 0:
            w = swa_w_qkv[j].astype(BF16)
            wo = swa_w_o[j].astype(BF16)
            q, kd, vd = _swa_proj(h, mod, g_mix[i], w, swa_g_q[j], swa_g_k[j], cos_a, sin_a, True)
            qc, kcd, vcd = map(per_batch,
                               _swa_proj(flat(hc), modc, g_mix[i], w, swa_g_q[j], swa_g_k[j], cos_a, sin_a, False))
            o = _swa_attn(q, kd, vd, kcd, vcd, swa_sink[j])
            if need_ctx:
                oc = _swa_attn(qc, None, None, kcd, vcd, swa_sink[j])
        elif m == 1:
            w = na_w_qkv[j].astype(BF16)
            wo = na_w_o[j].astype(BF16)
            q, k, v = _na_proj(h, mod, g_mix[i], w, na_g_q[j], na_g_k[j])
            qc, kc, vc = map(per_batch, _na_proj(flat(hc), modc, g_mix[i], w, na_g_q[j], na_g_k[j]))
            o = _na_attn(q, k, v, kc, vc, _na_bias_tiles(na_rpb[j], _na_tile_ids(L // GRID_W)[1]))
            if need_ctx:
                oc = _na_attn(qc, None, None, kc, vc, None)
        elif m == 2:
            pw = pool_w[j].astype(BF16)
            h = _pool(h, mod, g_mix[i], pw, pool_b[j], pool_scale[j])
            if need_ctx:
                hc = _pool(hc, modc, g_mix[i], pw, pool_b[j], pool_scale[j])
        else:
            wts = _mla_weights(mla_w_a[j], mla_g_cq[j], mla_g_ckv[j], mla_w_uq[j], mla_w_ukv[j],
                               mla_g_q[j], mla_g_k[j])
            wo = mla_w_o[j].astype(BF16)
            assert not need_ctx, "the latent-attention mixer is only built for the last layer"
            q, k, v = _mla_proj(h, mod, g_mix[i], wts, cos_d, sin_d, True, True)
            kc, vc = map(per_batch, _mla_proj(flat(hc), modc, g_mix[i], wts, cos_d, sin_d, False, False))
            o = _mla_attn(q, k, kc, v, vc)
        wgu = w_gate_up[i].astype(BF16)
        wd = w_down[i].astype(BF16)
        h = _tail(o, wo, h, mod, g_ffn[i], wgu, wd)
        if need_ctx:
            if oc is None:
                hc = per_batch(_tail(None, None, flat(hc), modc, g_ffn[i], wgu, wd))
            else:
                hc = per_batch(_tail(flat(oc), wo, flat(hc), modc, g_ffn[i], wgu, wd))
    return h
```
